```python
import math
import jax, jax.numpy as jnp
from jax import lax
import numpy as np


D_MODEL = 1024
BATCH = 8
SEQ = 2048
DEPTH = 2
DEC_BATCH = 32
DEC_SEQ = 8
PAST_LEN = 16384
PAGE_SIZE = 128

N_MIXERS = 4
GROUP_W = D_MODEL // N_MIXERS
A_HEADS = 4
A_DK = GROUP_W // A_HEADS
A_DV = GROUP_W // A_HEADS
A_CHUNK = 64
B_HEADS = 4
B_DV = GROUP_W // B_HEADS
B_DQ = B_DV // 2
C_GROUPS = 4
C_DG = GROUP_W // C_GROUPS
C_CHUNK = 128
D_HEADS = 4
D_DH = GROUP_W // D_HEADS
IDX_HEADS = 8
IDX_DIM = 32
TOPK_MAX = 256
REL_BUCKETS = 32
REL_EXACT = REL_BUCKETS // 2
REL_MAX_DIST = 128
QBLOCK = 128
D_FF = -(-8 * D_MODEL // (3 * 256)) * 256
EPS = 1e-6

SPLITS = (('a_q', A_HEADS * A_DK), ('a_f', A_HEADS * A_DK), ('a_i', A_HEADS * A_DV), ('a_g', A_HEADS * A_DV),
          ('b_q', B_HEADS * 2 * B_DQ), ('b_k', B_HEADS * 2 * B_DQ), ('b_v', B_HEADS * B_DV),
          ('c_u', GROUP_W), ('c_v', GROUP_W),
          ('d_q', D_HEADS * D_DH), ('d_k', D_HEADS * D_DH), ('d_v', D_HEADS * D_DH),
          ('d_qi', IDX_HEADS * IDX_DIM), ('d_ki', IDX_DIM), ('d_w', IDX_HEADS))
SPLIT_NAMES = tuple(n for n, _ in SPLITS)
SPLIT_IDX = tuple(int(i) for i in np.cumsum([w for _, w in SPLITS])[:-1])
D_IN = sum(w for _, w in SPLITS)

kernel_name = 'hybrid_hgrn2_diffattn_sgu_dsa_step'


def _rms(x, g):
    xf = x.astype(jnp.float32)
    y = xf * lax.rsqrt(jnp.mean(xf * xf, axis=-1, keepdims=True) + EPS)
    return (y * g.astype(jnp.float32)).astype(x.dtype)


def _rel_bucket(rel):
    n = jnp.maximum(rel, 0)
    nf = jnp.maximum(n, 1).astype(jnp.float32)
    large = REL_EXACT + (jnp.log(nf / REL_EXACT) / math.log(REL_MAX_DIST / REL_EXACT)
                         * (REL_BUCKETS - REL_EXACT)).astype(jnp.int32)
    large = jnp.minimum(large, REL_BUCKETS - 1)
    return jnp.where(n < REL_EXACT, n, large)


def _gather_pages(pool, l, page_table):
    g = pool[l, page_table]
    return g.reshape(g.shape[0], g.shape[1] * g.shape[2], *g.shape[3:])


def _sweep(fn, qpos, *qs):
    lq = qpos.shape[0]
    if lq <= QBLOCK:
        return fn(qpos, *qs)
    nb = -(-lq // QBLOCK)
    pad = nb * QBLOCK - lq
    qpos_b = jnp.pad(qpos, (0, pad), mode='edge').reshape(nb, QBLOCK)

    def blk(a):
        a = jnp.pad(a, [(0, 0), (0, pad)] + [(0, 0)] * (a.ndim - 2))
        return jnp.moveaxis(a.reshape(a.shape[0], nb, QBLOCK, *a.shape[2:]), 1, 0)

    out = lax.map(lambda args: fn(*args), (qpos_b,) + tuple(blk(a) for a in qs))
    out = jnp.moveaxis(out, 0, 1)
    out = out.reshape(out.shape[0], nb * QBLOCK, *out.shape[3:])
    return out[:, :lq]


def _hgrn2(q, k, v, lf, s0):
    bsz, L = q.shape[:2]
    c = min(A_CHUNK, L)
    n = -(-L // c)
    pad = n * c - L

    def prep(a):
        a = jnp.pad(a, [(0, 0), (0, pad), (0, 0), (0, 0)])
        return jnp.moveaxis(a.reshape(bsz, n, c, *a.shape[2:]), 1, 0)

    tri = jnp.tril(jnp.ones((c, c), bool))[None, :, :, None, None]

    def step(S, inp):
        qc, kc, vc, fc = inp
        bc = jnp.cumsum(fc, axis=1)
        o = jnp.einsum('bthk,bhkv->bthv', qc * jnp.exp(bc), S)
        dec = jnp.exp(jnp.where(tri, bc[:, :, None] - bc[:, None], -jnp.inf))
        att = jnp.einsum('bthk,bshk,btshk->bhts', qc, kc, dec)
        o = o + jnp.einsum('bhts,bshv->bthv', att, vc)
        blast = bc[:, -1]
        S = jnp.exp(blast)[..., None] * S + jnp.einsum('bshk,bshv->bhkv', kc * jnp.exp(blast[:, None] - bc), vc)
        return S, o

    S, o = lax.scan(step, s0, tuple(prep(a) for a in (q, k, v, lf)))
    o = jnp.moveaxis(o, 0, 1).reshape(bsz, n * c, q.shape[2], v.shape[3])[:, :L]
    return o, S


def _diff_attention(q, k, v, qpos, kpos, lam, bias_tab):
    def fn(qp, qb):
        s = jnp.einsum('bqhcd,bkhcd->bhcqk', qb, k).astype(jnp.float32) * (B_DQ ** -0.5)
        bias = jnp.transpose(bias_tab[_rel_bucket(qp[:, None] - kpos[None, :])], (2, 0, 1)).astype(jnp.float32)
        mask = kpos[None, :] <= qp[:, None]
        s = jnp.where(mask, s + bias[None, :, None], -jnp.inf)
        pr = jax.nn.softmax(s, axis=-1)
        a = pr[:, :, 0] - lam * pr[:, :, 1]
        return jnp.einsum('bhqk,bkhd->bqhd', a.astype(v.dtype), v)
    return _sweep(fn, qpos, q)


def _sparse_attention(q, qi, w, qpos, ki_all, kpos, gather_kv, bias_tab, n_sel):
    def fn(qp, qb, qib, wb):
        s = jnp.einsum('bqhd,bkd->bqhk', qib, ki_all).astype(jnp.float32)
        score = jnp.einsum('bqh,bqhk->bqk', wb.astype(jnp.float32), jax.nn.relu(s))
        score = jnp.where((kpos[None, :] <= qp[:, None])[None], score, -jnp.inf)
        _, idx = lax.top_k(score, n_sel)
        k_sel, v_sel = gather_kv(idx)
        valid = idx <= qp[None, :, None]
        logits = jnp.einsum('bqhd,bqkhd->bhqk', qb, k_sel).astype(jnp.float32) * (D_DH ** -0.5)
        bias = jnp.moveaxis(bias_tab[_rel_bucket(qp[None, :, None] - idx)], -1, 1).astype(jnp.float32)
        logits = jnp.where(valid[:, None], logits + bias, -jnp.inf)
        pr = jax.nn.softmax(logits, axis=-1)
        return jnp.einsum('bhqk,bqkhd->bqhd', pr.astype(v_sel.dtype), v_sel)
    return _sweep(fn, qpos, q, qi, w)


def _chunk_mlp(u, vn, w_s, b_s):
    bsz, L = u.shape[:2]
    n = -(-L // C_CHUNK)
    pad = n * C_CHUNK - L
    vp = jnp.pad(vn, ((0, 0), (0, pad), (0, 0))).reshape(bsz, n, C_CHUNK, C_GROUPS, C_DG)
    wm = jnp.where(jnp.tril(jnp.ones((C_CHUNK, C_CHUNK), bool))[None], w_s, 0)
    mixed = jnp.einsum('gts,bnsgc->bntgc', wm, vp) + jnp.transpose(b_s)[None, None, :, :, None]
    mixed = mixed.reshape(bsz, n * C_CHUNK, GROUP_W)[:, :L]
    return u * mixed.astype(u.dtype)


def _layer(x, l, prm, past):
    f32 = jnp.float32
    bsz, L, _ = x.shape
    pt = None if past is None else past['page_table']
    past_len = 0 if past is None else pt.shape[1] * PAGE_SIZE
    qpos = past_len + jnp.arange(L, dtype=jnp.int32)
    kpos = jnp.arange(past_len + L, dtype=jnp.int32)
    h = _rms(x, prm['attn_norm'][l])
    p = dict(zip(SPLIT_NAMES, jnp.split(h @ prm['w_in'][l], SPLIT_IDX, axis=-1)))

    lb_all = jnp.cumsum(jax.nn.softmax(prm['hgrn_lb_logits'].astype(f32), axis=0), axis=0)
    lb = (lb_all[l] - lb_all[0]).reshape(A_HEADS, A_DK)
    zf = p['a_f'].astype(f32).reshape(bsz, L, A_HEADS, A_DK)
    log_f = jnp.logaddexp(jnp.log(lb), jnp.log1p(-lb) + jax.nn.log_sigmoid(zf))
    a_k = -jnp.expm1(log_f)
    a_q = jax.nn.silu(p['a_q'].astype(f32)).reshape(bsz, L, A_HEADS, A_DK)
    a_v = p['a_i'].astype(f32).reshape(bsz, L, A_HEADS, A_DV)
    s0 = jnp.zeros((bsz, A_HEADS, A_DK, A_DV), f32) if past is None else past['hgrn'][l].astype(f32)
    o_a, s_a = _hgrn2(a_q, a_k, a_v, log_f, s0)
    g_a = jax.nn.silu(p['a_g'].reshape(bsz, L, A_HEADS, A_DV))
    y_a = (_rms(o_a.astype(x.dtype), prm['hgrn_norm'][l].reshape(A_HEADS, A_DV)) * g_a).reshape(bsz, L, GROUP_W)

    b_q = p['b_q'].reshape(bsz, L, B_HEADS, 2, B_DQ)
    b_k = p['b_k'].reshape(bsz, L, B_HEADS, 2 * B_DQ)
    b_v = p['b_v'].reshape(bsz, L, B_HEADS, B_DV)
    if past is None:
        b_k_all, b_v_all = b_k, b_v
    else:
        b_k_all = jnp.concatenate([_gather_pages(past['k_diff'], l, pt), b_k], axis=1)
        b_v_all = jnp.concatenate([_gather_pages(past['v_diff'], l, pt), b_v], axis=1)
    lam_init = 0.8 - 0.6 * math.exp(-0.3 * l)
    lam = (jnp.exp(jnp.sum(prm['lq1'][l].astype(f32) * prm['lk1'][l].astype(f32)))
           - jnp.exp(jnp.sum(prm['lq2'][l].astype(f32) * prm['lk2'][l].astype(f32))) + lam_init)
    o_b = _diff_attention(b_q, b_k_all.reshape(bsz, past_len + L, B_HEADS, 2, B_DQ), b_v_all,
                          qpos, kpos, lam, prm['rel_bias'][:, :B_HEADS])
    y_b = (_rms(o_b, prm['diff_norm'][l]) * (1.0 - lam_init)).reshape(bsz, L, GROUP_W)

    c_u = jax.nn.gelu(p['c_u'], approximate=False)
    c_v = _rms(jax.nn.gelu(p['c_v'], approximate=False), prm['sgu_norm'][l])
    y_c = _chunk_mlp(c_u, c_v, prm['sgu_w'][l], prm['sgu_b'][l])

    d_q = p['d_q'].reshape(bsz, L, D_HEADS, D_DH)
    d_k = p['d_k'].reshape(bsz, L, D_HEADS, D_DH)
    d_v = p['d_v'].reshape(bsz, L, D_HEADS, D_DH)
    d_qi = p['d_qi'].reshape(bsz, L, IDX_HEADS, IDX_DIM)
    d_ki = p['d_ki']
    d_w = p['d_w'] * (IDX_HEADS ** -0.5 * IDX_DIM ** -0.5)
    bi = jnp.arange(bsz)[:, None, None]
    if past is None:
        ki_all = d_ki

        def gather_kv(idx):
            return d_k[bi, idx], d_v[bi, idx]
    else:
        ki_all = jnp.concatenate([_gather_pages(past['k_index'], l, pt), d_ki], axis=1)

        def gather_kv(idx):
            in_past = (idx < past_len)[..., None, None]
            pi = jnp.minimum(idx, past_len - 1)
            phys = pt[bi, pi // PAGE_SIZE]
            off = pi % PAGE_SIZE
            ni = jnp.clip(idx - past_len, 0, L - 1)
            k_sel = jnp.where(in_past, past['k_sparse'][l, phys, off], d_k[bi, ni])
            v_sel = jnp.where(in_past, past['v_sparse'][l, phys, off], d_v[bi, ni])
            return k_sel, v_sel
    n_sel = min(TOPK_MAX, (past_len + L) // 4)
    o_d = _sparse_attention(d_q, d_qi, d_w, qpos, ki_all, kpos, gather_kv, prm['rel_bias'][:, B_HEADS:], n_sel)
    y_d = o_d.reshape(bsz, L, GROUP_W)

    y = jnp.concatenate([y_a, y_b, y_c, y_d], axis=-1) * prm['mix_scale'][l]
    x = x + y @ prm['w_out'][l]
    g, u = jnp.split(_rms(x, prm['ffn_norm'][l]) @ prm['w_gate_up'][l], 2, axis=-1)
    x = x + (jax.nn.silu(g) * u) @ prm['w_down'][l]
    return x, (b_k, b_v, d_k, d_v, d_ki, s_a.astype(x.dtype), c_v)


def setup_inputs(seed: int = 0) -> dict:
    key = jax.random.key(seed)
    ks = jax.random.split(key, 32)
    f32 = jnp.float32
    n_pages = PAST_LEN // PAGE_SIZE
    n_used = DEC_BATCH * n_pages
    n_pool = n_used + max(1, n_used // 4)

    def nrm(k, shape, s=1.0):
        return jax.random.normal(k, shape, f32) * s

    page_table = jax.random.permutation(ks[0], n_pool)[:n_used].reshape(DEC_BATCH, n_pages).astype(jnp.int32)
    return {
        'x_prompt': nrm(ks[1], (BATCH, SEQ, D_MODEL)),
        'x_sample': nrm(ks[2], (DEC_BATCH, DEC_SEQ, D_MODEL)),
        'cache_k_diff': nrm(ks[3], (DEPTH, n_pool, PAGE_SIZE, B_HEADS, 2 * B_DQ)),
        'cache_v_diff': nrm(ks[4], (DEPTH, n_pool, PAGE_SIZE, B_HEADS, B_DV)),
        'cache_k_sparse': nrm(ks[5], (DEPTH, n_pool, PAGE_SIZE, D_HEADS, D_DH)),
        'cache_v_sparse': nrm(ks[6], (DEPTH, n_pool, PAGE_SIZE, D_HEADS, D_DH)),
        'cache_k_index': nrm(ks[7], (DEPTH, n_pool, PAGE_SIZE, IDX_DIM)),
        'state_hgrn': nrm(ks[8], (DEPTH, DEC_BATCH, A_HEADS, A_DK, A_DV), 0.5),
        'page_table': page_table,
        'w_in': nrm(ks[9], (DEPTH, D_MODEL, D_IN), D_MODEL ** -0.5),
        'w_out': nrm(ks[10], (DEPTH, D_MODEL, D_MODEL), D_MODEL ** -0.5),
        'attn_norm': 1.0 + nrm(ks[11], (DEPTH, D_MODEL), 0.01),
        'ffn_norm': 1.0 + nrm(ks[12], (DEPTH, D_MODEL), 0.01),
        'final_norm': 1.0 + nrm(ks[13], (D_MODEL,), 0.01),
        'w_gate_up': nrm(ks[14], (DEPTH, D_MODEL, 2 * D_FF), D_MODEL ** -0.5),
        'w_down': nrm(ks[15], (DEPTH, D_FF, D_MODEL), D_FF ** -0.5),
        'hgrn_lb_logits': nrm(ks[16], (DEPTH, A_HEADS * A_DK), 0.5),
        'hgrn_norm': 1.0 + nrm(ks[17], (DEPTH, A_HEADS * A_DV), 0.01),
        'diff_lambda_q1': nrm(ks[18], (DEPTH, B_DQ), 0.1),
        'diff_lambda_k1': nrm(ks[19], (DEPTH, B_DQ), 0.1),
        'diff_lambda_q2': nrm(ks[20], (DEPTH, B_DQ), 0.1),
        'diff_lambda_k2': nrm(ks[21], (DEPTH, B_DQ), 0.1),
        'diff_norm': 1.0 + nrm(ks[22], (DEPTH, B_DV), 0.01),
        'sgu_w': nrm(ks[23], (DEPTH, C_GROUPS, C_CHUNK, C_CHUNK), C_CHUNK ** -0.5),
        'sgu_b': 1.0 + nrm(ks[24], (DEPTH, C_GROUPS, C_CHUNK), 0.01),
        'sgu_norm': 1.0 + nrm(ks[25], (DEPTH, GROUP_W), 0.01),
        'mix_scale': 1.0 + nrm(ks[26], (DEPTH, D_MODEL), 0.01),
        'rel_bias': nrm(ks[27], (REL_BUCKETS, B_HEADS + D_HEADS), 0.5),
    }


def reference(x_prompt, x_sample, cache_k_diff, cache_v_diff, cache_k_sparse, cache_v_sparse, cache_k_index,
              state_hgrn, page_table, w_in, w_out, attn_norm, ffn_norm, final_norm, w_gate_up, w_down,
              hgrn_lb_logits, hgrn_norm, diff_lambda_q1, diff_lambda_k1, diff_lambda_q2, diff_lambda_k2,
              diff_norm, sgu_w, sgu_b, sgu_norm, mix_scale, rel_bias):
    prm = dict(w_in=w_in, w_out=w_out, attn_norm=attn_norm, ffn_norm=ffn_norm, w_gate_up=w_gate_up,
               w_down=w_down, hgrn_lb_logits=hgrn_lb_logits, hgrn_norm=hgrn_norm, lq1=diff_lambda_q1,
               lk1=diff_lambda_k1, lq2=diff_lambda_q2, lk2=diff_lambda_k2, diff_norm=diff_norm, sgu_w=sgu_w,
               sgu_b=sgu_b, sgu_norm=sgu_norm, mix_scale=mix_scale, rel_bias=rel_bias)
    past = dict(page_table=page_table, k_diff=cache_k_diff, v_diff=cache_v_diff, k_sparse=cache_k_sparse,
                v_sparse=cache_v_sparse, k_index=cache_k_index, hgrn=state_hgrn)
    xp, xs = x_prompt, x_sample
    st_p, st_s = [], []
    for l in range(DEPTH):
        xp, sp = _layer(xp, l, prm, None)
        xs, ss = _layer(xs, l, prm, past)
        st_p.append(sp)
        st_s.append(ss)
    y_prompt = _rms(xp, final_norm)
    y_sample = _rms(xs, final_norm)

    def stk(lst, i):
        return jnp.stack([s[i] for s in lst], axis=0)

    return (y_prompt, y_sample,
            stk(st_p, 0), stk(st_p, 1), stk(st_p, 2), stk(st_p, 3), stk(st_p, 4), stk(st_p, 5),
            stk(st_s, 0), stk(st_s, 1), stk(st_s, 2), stk(st_s, 3), stk(st_s, 4), stk(st_s, 5), stk(st_s, 6))
```

```python
import functools
import math

import numpy as np
import jax
import jax.numpy as jnp
from jax import lax
from jax.experimental import pallas as pl
from jax.experimental.pallas import tpu as pltpu

F32 = jnp.float32
BF16 = jnp.bfloat16
I32 = jnp.int32

D_MODEL = 1024
GROUP_W = 256
N_HEADS = 4
HEAD_DV = 64
DIFF_DQ = 32
IDX_HEADS = 8
IDX_DIM = 32
TOPK_MAX = 256
REL_BUCKETS = 32
REL_EXACT = 16
REL_MAX_DIST = 128
PAGE = 128
D_FF = 2816
EPS = 1e-6
D_IN = 3368
D_IN_PAD = 3456
COL = {'a_q': 0, 'a_f': 1, 'a_i': 2, 'a_g': 3, 'b_q': 4, 'b_k': 5, 'b_v': 6, 'c_u': 7, 'c_v': 8,
       'd_q': 9, 'd_k': 10, 'd_v': 11, 'd_qi': 12}
TAIL_COL128 = 26
NEG = -1e30
INT_MIN = -2 ** 31
TQ = 128
HG = 16
VMEM_LIMIT = 56 * 1024 * 1024


def _cparams(sem):
    return pltpu.CompilerParams(dimension_semantics=sem, vmem_limit_bytes=VMEM_LIMIT)


def _dot(a, b):
    return jnp.dot(a, b, preferred_element_type=F32)


def _dot_nt(a, b):
    return lax.dot_general(a, b, (((1,), (1,)), ((), ())), preferred_element_type=F32)


def _dot_tn(a, b):
    return lax.dot_general(a, b, (((0,), (0,)), ((), ())), preferred_element_type=F32)


def _sigmoid(x):
    return 1.0 / (1.0 + jnp.exp(-x))


def _split3(x):
    hi = x.astype(BF16)
    r = x - hi.astype(F32)
    mid = r.astype(BF16)
    lo = (r - mid.astype(F32)).astype(BF16)
    return hi, mid, lo


def _dot_exact_lhs(m_bf16, x):
    hi, mid, lo = _split3(x)
    return _dot(m_bf16, hi) + _dot(m_bf16, mid) + _dot(m_bf16, lo)


def _dot_exact_rhs(x, m_bf16):
    hi, mid, lo = _split3(x)
    return _dot(hi, m_bf16) + _dot(mid, m_bf16) + _dot(lo, m_bf16)


def _proj_body(x_ref, g_ref, w_ref, o_ref):
    x = x_ref[...]
    h = x * lax.rsqrt(jnp.mean(x * x, axis=-1, keepdims=True) + EPS) * g_ref[...]
    o_ref[...] = _dot(h.astype(BF16), w_ref[...])


def _proj(x2d, gain, w_bf16):
    n, d = x2d.shape
    dn = w_bf16.shape[1]
    tm = min(256, n)
    return pl.pallas_call(
        _proj_body,
        grid=(n // tm,),
        in_specs=[pl.BlockSpec((tm, d), lambda i: (i, 0)),
                  pl.BlockSpec((1, d), lambda i: (0, 0)),
                  pl.BlockSpec((d, dn), lambda i: (0, 0))],
        out_specs=pl.BlockSpec((tm, dn), lambda i: (i, 0)),
        out_shape=jax.ShapeDtypeStruct((n, dn), F32),
        compiler_params=_cparams(("parallel",)),
        name="proj",
    )(x2d, gain.reshape(1, d), w_bf16)


FF_CHUNK = 256


def _outproj_body(x_ref, ya_ref, yb_ref, yc_ref, yd_ref, ms_ref, wo_ref, o_ref):
    acc = x_ref[...]
    for i, y_ref in enumerate((ya_ref, yb_ref, yc_ref, yd_ref)):
        lo, hi = GROUP_W * i, GROUP_W * (i + 1)
        yi = (y_ref[...] * ms_ref[:, lo:hi]).astype(BF16)
        acc = acc + _dot(yi, wo_ref[lo:hi, :])
    o_ref[...] = acc


def _ffn_body(x_ref, fn_ref, wgu_ref, wd_ref, fin_ref, o_ref, acc_ref, *, final):
    x = x_ref[...]
    acc_ref[...] = x
    h = (x * lax.rsqrt(jnp.mean(x * x, axis=-1, keepdims=True) + EPS) * fn_ref[...]).astype(BF16)
    for c in range(D_FF // FF_CHUNK):
        lo, hi = FF_CHUNK * c, FF_CHUNK * (c + 1)
        g = _dot(h, wgu_ref[:, lo:hi])
        u = _dot(h, wgu_ref[:, D_FF + lo:D_FF + hi])
        a = (g * _sigmoid(g) * u).astype(BF16)
        acc_ref[...] += _dot(a, wd_ref[lo:hi, :])
    out = acc_ref[...]
    if final:
        out = out * lax.rsqrt(jnp.mean(out * out, axis=-1, keepdims=True) + EPS) * fin_ref[...]
    o_ref[...] = out


def _post(x2d, ys, mix_scale, wo, ffn_norm, wgu, wd, final_norm, final):
    n, d = x2d.shape
    tm = min(256, n)
    row = lambda i: (i, 0)
    const = lambda i: (0, 0)
    x1 = pl.pallas_call(
        _outproj_body,
        grid=(n // tm,),
        in_specs=[pl.BlockSpec((tm, d), row)]
        + [pl.BlockSpec((tm, GROUP_W), row) for _ in range(4)]
        + [pl.BlockSpec((1, d), const), pl.BlockSpec((d, d), const)],
        out_specs=pl.BlockSpec((tm, d), row),
        out_shape=jax.ShapeDtypeStruct((n, d), F32),
        compiler_params=_cparams(("parallel",)),
        name="outproj",
    )(x2d, *ys, mix_scale.reshape(1, d), wo)
    return pl.pallas_call(
        functools.partial(_ffn_body, final=final),
        grid=(n // tm,),
        in_specs=[pl.BlockSpec((tm, d), row),
                  pl.BlockSpec((1, d), const),
                  pl.BlockSpec((d, 2 * D_FF), const),
                  pl.BlockSpec((D_FF, d), const),
                  pl.BlockSpec((1, d), const)],
        out_specs=pl.BlockSpec((tm, d), row),
        out_shape=jax.ShapeDtypeStruct((n, d), F32),
        scratch_shapes=[pltpu.VMEM((tm, d), F32)],
        compiler_params=_cparams(("parallel",)),
        name="ffn",
    )(x1, ffn_norm.reshape(1, d), wgu, wd, final_norm.reshape(1, d))


def _hgrn_consts(tr):
    r = np.arange(tr)
    same = (r[:, None] // HG) == (r[None, :] // HG)
    tri = same & (r[None, :] <= r[:, None])
    c = np.arange(GROUP_W)
    heads = (c[:, None] // HEAD_DV) == (c[None, :] // HEAD_DV)
    return (jnp.asarray(tri, BF16), jnp.asarray(same, BF16), jnp.asarray(heads, BF16), jnp.asarray(heads, F32))


def _shift_rows(x, d, row):
    if d == 0:
        return x
    return jnp.where(row >= d, pltpu.roll(x, d, axis=0), 0.0)


def _hgrn_body(aq_ref, af_ref, ai_ref, ag_ref, la_ref, l1_ref, hn_ref, tri_ref, same_ref, e_ref, mask_ref, st0_ref,
               ya_ref, st_ref, cum_ref, tot_ref, kk_ref, qq_ref, qe_ref, ke_ref, o_ref, *, seq, tr, valid_rows):
    st_ref[...] = st0_ref[...]

    def prep(t, carry):
        r0 = pl.multiple_of(t * tr, tr)
        rows = pl.ds(r0, tr)
        z = af_ref[rows, :]
        ls = jnp.minimum(z, 0.0) - jnp.log(1.0 + jnp.exp(-jnp.abs(z)))
        b = l1_ref[...] + ls
        a = la_ref[...]
        lf = jnp.maximum(a, b) + jnp.log(1.0 + jnp.exp(-jnp.abs(a - b)))
        kk = 1.0 - jnp.exp(lf)
        if valid_rows < seq:
            ok = (lax.broadcasted_iota(I32, (tr, GROUP_W), 0) + r0) < valid_rows
            lf = jnp.where(ok, lf, 0.0)
            kk = jnp.where(ok, kk, 0.0)
        cum = _dot_exact_lhs(tri_ref[...], lf)
        tot = _dot_exact_lhs(same_ref[...], lf)
        q = aq_ref[rows, :]
        qq = q * _sigmoid(q)
        cum_ref[rows, :] = cum
        tot_ref[rows, :] = tot
        kk_ref[rows, :] = kk
        qq_ref[rows, :] = qq
        qe_ref[rows, :] = (qq * jnp.exp(cum)).astype(BF16)
        ke_ref[rows, :] = (kk * jnp.exp(tot - cum)).astype(BF16)
        return carry

    lax.fori_loop(0, seq // tr, prep, 0)

    row = lax.broadcasted_iota(I32, (HG, GROUP_W), 0)

    def block(i, carry):
        rows = pl.ds(pl.multiple_of(i * HG, HG), HG)
        cum = cum_ref[rows, :]
        kk = kk_ref[rows, :]
        qq = qq_ref[rows, :]
        v = ai_ref[rows, :]
        o = _dot_nt(qe_ref[rows, :], st_ref[...].astype(BF16))
        for d in range(HG):
            x = qq * _shift_rows(kk, d, row) * jnp.exp(cum - _shift_rows(cum, d, row))
            o = o + _dot(x.astype(BF16), e_ref[...]) * _shift_rows(v, d, row)
        o_ref[rows, :] = o
        dec = jnp.exp(tot_ref[pl.ds(pl.multiple_of(i * HG, HG), 1), :])
        upd = _dot_tn(v.astype(BF16), ke_ref[rows, :])
        st_ref[...] = st_ref[...] * dec + mask_ref[...] * upd
        return carry

    lax.fori_loop(0, seq // HG, block, 0)

    def fin(t, carry):
        rows = pl.ds(pl.multiple_of(t * tr, tr), tr)
        o = o_ref[rows, :]
        hi = (o * o).astype(BF16)
        lo = (o * o - hi.astype(F32)).astype(BF16)
        msq = (_dot(hi, e_ref[...]) + _dot(lo, e_ref[...])) * (1.0 / HEAD_DV)
        g = ag_ref[rows, :]
        ya_ref[rows, :] = o * lax.rsqrt(msq + EPS) * hn_ref[...] * (g * _sigmoid(g))
        return carry

    lax.fori_loop(0, seq // tr, fin, 0)


def _hgrn(p2d, bsz, seq, log_lb, log1m_lb, hnorm, st0, valid_rows):
    tr = min(128, seq)
    tri, same, e_bf, mask = _hgrn_consts(tr)
    col = lambda name: pl.BlockSpec((seq, GROUP_W), lambda b, c=COL[name]: (b, c))
    vec = pl.BlockSpec((1, GROUP_W), lambda b: (0, 0))
    mat = lambda n: pl.BlockSpec((n, n), lambda b: (0, 0))
    st_spec = pl.BlockSpec((None, GROUP_W, GROUP_W), lambda b: (b, 0, 0))
    big = lambda dt: pltpu.VMEM((seq, GROUP_W), dt)
    return pl.pallas_call(
        functools.partial(_hgrn_body, seq=seq, tr=tr, valid_rows=valid_rows),
        grid=(bsz,),
        in_specs=[col('a_q'), col('a_f'), col('a_i'), col('a_g'), vec, vec, vec, mat(tr), mat(tr),
                  mat(GROUP_W), mat(GROUP_W), st_spec],
        out_specs=[pl.BlockSpec((seq, GROUP_W), lambda b: (b, 0)), st_spec],
        out_shape=[jax.ShapeDtypeStruct((bsz * seq, GROUP_W), F32),
                   jax.ShapeDtypeStruct((bsz, GROUP_W, GROUP_W), F32)],
        scratch_shapes=[big(F32), big(F32), big(F32), big(F32), big(BF16), big(BF16), big(F32)],
        compiler_params=_cparams(("parallel",)),
        name="hgrn",
    )(p2d, p2d, p2d, p2d, log_lb.reshape(1, GROUP_W), log1m_lb.reshape(1, GROUP_W), hnorm.reshape(1, GROUP_W),
      tri, same, e_bf, mask, st0)


def _state_to_blockdiag(s):
    st = jnp.swapaxes(s, 2, 3)
    eye = jnp.eye(N_HEADS, dtype=s.dtype)
    return jnp.einsum('bhvk,hg->bhvgk', st, eye).reshape(s.shape[0], GROUP_W, GROUP_W)


def _blockdiag_to_state(st):
    b = st.shape[0]
    st5 = st.reshape(b, N_HEADS, HEAD_DV, N_HEADS, HEAD_DV)
    diag = jnp.stack([st5[:, h, :, h, :] for h in range(N_HEADS)], axis=1)
    return jnp.swapaxes(diag, 2, 3)


def _gelu(x):
    return 0.5 * x * (1.0 + lax.erf(x * (2.0 ** -0.5)))


def _sgu_body(cu_ref, cv_ref, sn_ref, mix_ref, bias_ref, yc_ref, cvn_ref, *, tr):
    u = _gelu(cu_ref[...])
    v = _gelu(cv_ref[...])
    vn = v * lax.rsqrt(jnp.mean(v * v, axis=-1, keepdims=True) + EPS) * sn_ref[...]
    cvn_ref[...] = vn
    vb = vn.astype(BF16)
    lane = lax.broadcasted_iota(I32, (tr, GROUP_W), 1)
    mixed = bias_ref[...]
    for g in range(N_HEADS):
        mg = _dot(mix_ref[g], vb)
        mixed = mixed + jnp.where((lane >= HEAD_DV * g) & (lane < HEAD_DV * (g + 1)), mg, 0.0)
    yc_ref[...] = u * mixed


def _sgu(p2d, snorm, mix_bf16, bias_tile):
    n = p2d.shape[0]
    tr = 128
    col = lambda name: pl.BlockSpec((tr, GROUP_W), lambda i, c=COL[name]: (i, c))
    out = pl.BlockSpec((tr, GROUP_W), lambda i: (i, 0))
    return pl.pallas_call(
        functools.partial(_sgu_body, tr=tr),
        grid=(n // tr,),
        in_specs=[col('c_u'), col('c_v'),
                  pl.BlockSpec((1, GROUP_W), lambda i: (0, 0)),
                  pl.BlockSpec((N_HEADS, tr, tr), lambda i: (0, 0, 0)),
                  pl.BlockSpec((tr, GROUP_W), lambda i: (0, 0))],
        out_specs=[out, out],
        out_shape=[jax.ShapeDtypeStruct((n, GROUP_W), F32)] * 2,
        compiler_params=_cparams(("parallel",)),
        name="sgu",
    )(p2d, p2d, snorm.reshape(1, GROUP_W), mix_bf16, bias_tile)


def _sgu_params(w, b, seq):
    c = min(128, seq)
    wt = jnp.tril(w[:, :c, :c])
    mix = jnp.einsum('rs,gtu->grtsu', jnp.eye(128 // c, dtype=w.dtype), wt).reshape(N_HEADS, 128, 128)
    bt = jnp.tile(jnp.transpose(b[:, :c]), (128 // c, 1))
    return mix.astype(BF16), jnp.repeat(bt, HEAD_DV, axis=1)


def _rel_bucket(rel):
    n = jnp.maximum(rel, 0)
    nf = jnp.maximum(n, 1).astype(F32)
    large = REL_EXACT + (jnp.log(nf / REL_EXACT) / math.log(REL_MAX_DIST / REL_EXACT)
                         * (REL_BUCKETS - REL_EXACT)).astype(I32)
    large = jnp.minimum(large, REL_BUCKETS - 1)
    return jnp.where(n < REL_EXACT, n, large)


def _bias_tiles(tab):
    assert REL_MAX_DIST <= TQ
    q = np.arange(TQ)[:, None]
    k = np.arange(TQ)[None, :]
    tiles = []
    for back in range(3):
        rel = jnp.asarray(q - k + TQ * back, I32)
        t = jnp.transpose(tab[_rel_bucket(rel)], (2, 0, 1)).astype(F32)
        if back == 0:
            t = jnp.where(jnp.asarray(k <= q)[None], t, NEG)
        tiles.append(t)
    return jnp.stack(tiles, axis=0)


def _lane_band(x, lo, width):
    lane = lax.broadcasted_iota(I32, x.shape, x.ndim - 1)
    return jnp.where((lane >= lo) & (lane < lo + width), x, 0.0)


def _softmax_step(s, m_ref, acc_ref, idx, v):
    m_old = m_ref[idx]
    m_new = jnp.maximum(m_old, jnp.max(s, axis=-1, keepdims=True))
    alpha = jnp.exp(m_old - m_new)
    p = jnp.exp(s - m_new)
    acc_ref[idx] = alpha * acc_ref[idx] + _dot(p.astype(BF16), v)
    m_ref[idx] = m_new


def _normalised(acc):
    return acc / pltpu.roll(acc, HEAD_DV, axis=1)


def _pack_heads(o_ref, res):
    lane = lax.broadcasted_iota(I32, res[0].shape, 1)
    for pair in range(2):
        both = jnp.where(lane < HEAD_DV, res[2 * pair], pltpu.roll(res[2 * pair + 1], HEAD_DV, axis=1))
        o_ref[:, 2 * HEAD_DV * pair:2 * HEAD_DV * (pair + 1)] = both


def _diff_body(lam_ref, q_ref, k_ref, v_ref, bias_ref, gain_ref, o_ref, qm_ref, m_ref, acc_ref, *, out_scale):
    i = pl.program_id(1)
    q = q_ref[...] * (DIFF_DQ ** -0.5)
    for hc in range(2 * N_HEADS):
        qm_ref[hc] = _lane_band(q, DIFF_DQ * hc, DIFF_DQ).astype(BF16)
    m_ref[...] = jnp.full(m_ref.shape, NEG, F32)
    acc_ref[...] = jnp.zeros(acc_ref.shape, F32)

    def step(j, carry):
        rows = pl.ds(pl.multiple_of(j * TQ, TQ), TQ)
        kj = k_ref[rows, :]
        back = jnp.minimum(i - j, 2)
        for hc in range(2 * N_HEADS):
            h = hc // 2
            s = _dot_nt(qm_ref[hc], kj) + bias_ref[back, h]
            _softmax_step(s, m_ref, acc_ref, hc, v_ref[h, rows, :])
        return carry

    lax.fori_loop(0, i + 1, step, 0)

    lam = lam_ref[0]
    lane = lax.broadcasted_iota(I32, (TQ, 2 * HEAD_DV), 1)
    res = []
    for h in range(N_HEADS):
        o = _normalised(acc_ref[2 * h]) - lam * _normalised(acc_ref[2 * h + 1])
        msq = jnp.sum(jnp.where(lane < HEAD_DV, o * o, 0.0), axis=-1, keepdims=True) * (1.0 / HEAD_DV)
        res.append(o * lax.rsqrt(msq + EPS) * gain_ref[...] * out_scale)
    _pack_heads(o_ref, res)


def _value_heads(v2d, bsz, seq):
    v = jnp.transpose(v2d.reshape(bsz, seq, N_HEADS, HEAD_DV), (0, 2, 1, 3))
    return jnp.concatenate([v, jnp.ones_like(v)], axis=-1).astype(BF16)


def _diff_prompt(p2d, bsz, seq, k_bf16, v_heads, lam, bias, gain, out_scale):
    nq = seq // TQ
    return pl.pallas_call(
        functools.partial(_diff_body, out_scale=out_scale),
        grid=(bsz, nq),
        in_specs=[pl.BlockSpec(memory_space=pltpu.SMEM),
                  pl.BlockSpec((TQ, GROUP_W), lambda b, i: (b * nq + i, COL['b_q'])),
                  pl.BlockSpec((None, seq, GROUP_W), lambda b, i: (b, 0, 0)),
                  pl.BlockSpec((None, N_HEADS, seq, 2 * HEAD_DV), lambda b, i: (b, 0, 0, 0)),
                  pl.BlockSpec((3, N_HEADS, TQ, TQ), lambda b, i: (0, 0, 0, 0)),
                  pl.BlockSpec((1, 2 * HEAD_DV), lambda b, i: (0, 0))],
        out_specs=pl.BlockSpec((TQ, GROUP_W), lambda b, i: (b * nq + i, 0)),
        out_shape=jax.ShapeDtypeStruct((bsz * seq, GROUP_W), F32),
        scratch_shapes=[pltpu.VMEM((2 * N_HEADS, TQ, GROUP_W), BF16),
                        pltpu.VMEM((2 * N_HEADS, TQ, 2 * HEAD_DV), F32),
                        pltpu.VMEM((2 * N_HEADS, TQ, 2 * HEAD_DV), F32)],
        compiler_params=_cparams(("parallel", "parallel")),
        name="diff_prompt",
    )(lam.reshape(1), p2d, k_bf16, v_heads, bias, jnp.tile(gain, 2).reshape(1, 2 * HEAD_DV))


def _sort_key(score):
    bits = lax.bitcast_convert_type(score, I32)
    return jnp.where(bits < 0, bits ^ 0x7FFFFFFF, bits)


def _kth_largest(count_ge, n_sel, shape):
    t = jnp.full(shape, INT_MIN, I32)
    for bit in range(31, -1, -1):
        step = INT_MIN if bit == 31 else (1 << bit)
        cand = t + jnp.int32(step)
        t = jnp.where(count_ge(cand) >= n_sel, cand, t)
    return t


def _dsa_body(q_ref, qi_ref, tail_ref, k_ref, v_ref, ki_ref, bias_ref, tri_ref, o_ref,
              qm_ref, qim_ref, wb_ref, key_ref, sel_ref, m_ref, acc_ref, *, n_sel):
    i = pl.program_id(1)
    q = q_ref[...] * (HEAD_DV ** -0.5)
    for h in range(N_HEADS):
        qm_ref[h] = _lane_band(q, HEAD_DV * h, HEAD_DV).astype(BF16)
    qi = qi_ref[...]
    tail = tail_ref[...]
    w_scale = IDX_HEADS ** -0.5 * IDX_DIM ** -0.5
    for e in range(IDX_HEADS):
        qim_ref[e] = _lane_band(qi, IDX_DIM * e, IDX_DIM).astype(BF16)
        wb_ref[e] = jnp.broadcast_to(tail[:, IDX_DIM + e:IDX_DIM + e + 1] * w_scale, (TQ, TQ))
    m_ref[...] = jnp.full(m_ref.shape, NEG, F32)
    acc_ref[...] = jnp.zeros(acc_ref.shape, F32)
    row = lax.broadcasted_iota(I32, (TQ, TQ), 0)
    colk = lax.broadcasted_iota(I32, (TQ, TQ), 1)

    def score(j, carry):
        kij = ki_ref[pl.ds(pl.multiple_of(j * TQ, TQ), TQ), :]
        sc = jnp.zeros((TQ, TQ), F32)
        for e in range(IDX_HEADS):
            sc = sc + wb_ref[e] * jnp.maximum(_dot_nt(qim_ref[e], kij), 0.0)
        causal = (colk + j * TQ) <= (row + i * TQ)
        key_ref[j] = jnp.where(causal, _sort_key(sc), INT_MIN)
        return carry

    lax.fori_loop(0, i + 1, score, 0)

    def count(pred):
        def body(j, c):
            return c + jnp.where(pred(key_ref[j]), 1.0, 0.0)
        return jnp.sum(lax.fori_loop(0, i + 1, body, jnp.zeros((TQ, TQ), F32)), axis=-1, keepdims=True)

    thr = _kth_largest(lambda cand: count(lambda k: k >= cand), float(n_sel), (TQ, 1))
    room = float(n_sel) - count(lambda k: k > thr)

    def select(j, seen):
        k = key_ref[j]
        eq = jnp.where(k == thr, 1.0, 0.0)
        rank = seen + _dot(eq.astype(BF16), tri_ref[...])
        keep = (k > thr) | ((k == thr) & (rank <= room))
        sel_ref[j] = jnp.where(keep, 0.0, NEG)
        return seen + jnp.sum(eq, axis=-1, keepdims=True)

    lax.fori_loop(0, i + 1, select, jnp.zeros((TQ, 1), F32))

    def attend(j, carry):
        rows = pl.ds(pl.multiple_of(j * TQ, TQ), TQ)
        kj = k_ref[rows, :]
        back = jnp.minimum(i - j, 2)
        for h in range(N_HEADS):
            s = _dot_nt(qm_ref[h], kj) + bias_ref[back, h] + sel_ref[j]
            _softmax_step(s, m_ref, acc_ref, h, v_ref[h, rows, :])
        return carry

    lax.fori_loop(0, i + 1, attend, 0)
    _pack_heads(o_ref, [_normalised(acc_ref[h]) for h in range(N_HEADS)])


def _dsa_prompt(p2d, bsz, seq, k_bf16, v_heads, ki_rep, bias):
    nq = seq // TQ
    n_sel = min(TOPK_MAX, seq // 4)
    tri = jnp.asarray(np.arange(TQ)[:, None] <= np.arange(TQ)[None, :], BF16)
    qcol = lambda c: pl.BlockSpec((TQ, GROUP_W), lambda b, i, c=c: (b * nq + i, c))
    full = pl.BlockSpec((None, seq, GROUP_W), lambda b, i: (b, 0, 0))
    return pl.pallas_call(
        functools.partial(_dsa_body, n_sel=n_sel),
        grid=(bsz, nq),
        in_specs=[qcol(COL['d_q']), qcol(COL['d_qi']),
                  pl.BlockSpec((TQ, 128), lambda b, i: (b * nq + i, TAIL_COL128)),
                  full,
                  pl.BlockSpec((None, N_HEADS, seq, 2 * HEAD_DV), lambda b, i: (b, 0, 0, 0)),
                  full,
                  pl.BlockSpec((3, N_HEADS, TQ, TQ), lambda b, i: (0, 0, 0, 0)),
                  pl.BlockSpec((TQ, TQ), lambda b, i: (0, 0))],
        out_specs=pl.BlockSpec((TQ, GROUP_W), lambda b, i: (b * nq + i, 0)),
        out_shape=jax.ShapeDtypeStruct((bsz * seq, GROUP_W), F32),
        scratch_shapes=[pltpu.VMEM((N_HEADS, TQ, GROUP_W), BF16),
                        pltpu.VMEM((IDX_HEADS, TQ, GROUP_W), BF16),
                        pltpu.VMEM((IDX_HEADS, TQ, TQ), F32),
                        pltpu.VMEM((nq, TQ, TQ), I32),
                        pltpu.VMEM((nq, TQ, TQ), F32),
                        pltpu.VMEM((N_HEADS, TQ, 2 * HEAD_DV), F32),
                        pltpu.VMEM((N_HEADS, TQ, 2 * HEAD_DV), F32)],
        compiler_params=_cparams(("parallel", "parallel")),
        name="dsa_prompt",
    )(p2d, p2d, p2d, k_bf16, v_heads, ki_rep, bias, tri)


DEC_Q = 8
PPS = 8


def _paged_update(s, v, m_ref, l_ref, acc_ref):
    m_old = m_ref[...]
    m_new = jnp.maximum(m_old, jnp.max(s, axis=-1, keepdims=True))
    alpha = jnp.exp(m_old - m_new)
    p = jnp.exp(s - m_new)
    l_ref[...] = alpha * l_ref[...] + p
    acc_ref[...] = jnp.concatenate([alpha, alpha], axis=1) * acc_ref[...] + _dot(p.astype(BF16), v)
    m_ref[...] = m_new


def _paged_body(pt_ref, lam_ref, wq_ref, add_ref, knew_ref, vnew_ref, gain_ref, e_ref, *rest,
                n_pages, rows, diff, use_sel, out_scale):
    del pt_ref
    if use_sel:
        sel_ref, selnew_ref = rest[:2]
        rest = rest[2:]
    k_refs, v_refs = rest[:PPS], rest[PPS:2 * PPS]
    o_ref, m_ref, l_ref, acc_ref = rest[2 * PPS:]
    g = pl.program_id(1)

    @pl.when(g == 0)
    def _():
        m_ref[...] = jnp.full(m_ref.shape, NEG, F32)
        l_ref[...] = jnp.zeros(l_ref.shape, F32)
        acc_ref[...] = jnp.zeros(acc_ref.shape, F32)

    wq = wq_ref[...]
    for i in range(PPS):
        page = g * PPS + i
        s = _dot_nt(wq, k_refs[i][...].astype(BF16)) + add_ref[jnp.where(page == n_pages - 1, 1, 0)]
        if use_sel:
            s = s + jnp.tile(sel_ref[i], (rows // DEC_Q, 1))
        _paged_update(s, v_refs[i][...].astype(BF16), m_ref, l_ref, acc_ref)

    @pl.when(g == n_pages // PPS - 1)
    def _():
        s = _dot_nt(wq, knew_ref[...]) + add_ref[2]
        if use_sel:
            s = s + jnp.tile(selnew_ref[0], (rows // DEC_Q, 1))
        _paged_update(s, vnew_ref[...], m_ref, l_ref, acc_ref)
        o_all = acc_ref[...] / jnp.sum(l_ref[...], axis=-1, keepdims=True)
        out = jnp.zeros((DEC_Q, GROUP_W), F32)
        for h in range(N_HEADS):
            if diff:
                lo = 2 * h * DEC_Q
                o_h = o_all[lo:lo + DEC_Q] - lam_ref[0] * o_all[lo + DEC_Q:lo + 2 * DEC_Q]
            else:
                o_h = o_all[h * DEC_Q:(h + 1) * DEC_Q]
            out = out + _lane_band(o_h, HEAD_DV * h, HEAD_DV)
        if diff:
            sq = out * out
            hi = sq.astype(BF16)
            lo_ = (sq - hi.astype(F32)).astype(BF16)
            msq = (_dot(hi, e_ref[...]) + _dot(lo_, e_ref[...])) * (1.0 / HEAD_DV)
            out = out * lax.rsqrt(msq + EPS) * gain_ref[...] * out_scale
        o_ref[...] = out


def _paged_attn(page_table, lam, wq, add, k_new, v_new, gain256, sel, cache_k, cache_v, l, *, diff, out_scale):
    nreq, n_pages = page_table.shape
    rows = wq.shape[1]
    n_pool = cache_k.shape[1]
    ck = cache_k.reshape(cache_k.shape[0], n_pool, PAGE, GROUP_W)
    cv = cache_v.reshape(cache_v.shape[0], n_pool, PAGE, GROUP_W)
    use_sel = sel is not None
    page_spec = lambda i: pl.BlockSpec((None, None, PAGE, GROUP_W),
                                       lambda r, g, pt, i=i: (l, pt[r * n_pages + g * PPS + i], 0, 0))
    per_req = lambda shape: pl.BlockSpec((None,) + shape, lambda r, g, pt: (r,) + (0,) * len(shape))
    const = lambda shape: pl.BlockSpec(shape, lambda r, g, pt: (0,) * len(shape))
    in_specs = [pl.BlockSpec(memory_space=pltpu.SMEM), per_req((rows, GROUP_W)), const((3, rows, PAGE)),
                per_req((PAGE, GROUP_W)), per_req((PAGE, GROUP_W)), const((1, GROUP_W)), const((GROUP_W, GROUP_W))]
    args = [lam.reshape(1), wq, add, k_new, v_new, gain256.reshape(1, GROUP_W), _hgrn_consts(128)[2]]
    if use_sel:
        in_specs += [pl.BlockSpec((None, PPS, DEC_Q, PAGE), lambda r, g, pt: (r, g, 0, 0)),
                     pl.BlockSpec((None, 1, DEC_Q, PAGE), lambda r, g, pt: (r, n_pages, 0, 0))]
        args += [sel, sel]
    in_specs += [page_spec(i) for i in range(PPS)] * 2
    args += [ck] * PPS + [cv] * PPS
    return pl.pallas_call(
        functools.partial(_paged_body, n_pages=n_pages, rows=rows, diff=diff, use_sel=use_sel, out_scale=out_scale),
        grid_spec=pltpu.PrefetchScalarGridSpec(
            num_scalar_prefetch=1,
            grid=(nreq, n_pages // PPS),
            in_specs=in_specs,
            out_specs=pl.BlockSpec((DEC_Q, GROUP_W), lambda r, g, pt: (r, 0)),
            scratch_shapes=[pltpu.VMEM((rows, PAGE), F32), pltpu.VMEM((rows, PAGE), F32),
                            pltpu.VMEM((rows, GROUP_W), F32)]),
        out_shape=jax.ShapeDtypeStruct((nreq * DEC_Q, GROUP_W), F32),
        compiler_params=_cparams(("parallel", "arbitrary")),
        name="paged_diff" if diff else "paged_dsa",
    )(page_table.reshape(-1), *args)


def _decode_add_tiles(tab, past, groups_per_head):
    t = np.arange(DEC_Q)[:, None]
    c = np.arange(PAGE)[None, :]
    far = jnp.asarray(np.broadcast_to(past + t - (past - PAGE - 1), (DEC_Q, PAGE)), I32)
    last = jnp.asarray(t + PAGE - c, I32)
    new = jnp.asarray(t - c, I32)
    ok_new = jnp.asarray((c <= t) & (c < DEC_Q))
    tiles = []
    for rel, ok in ((far, None), (last, None), (new, ok_new)):
        b = jnp.transpose(tab[_rel_bucket(rel)], (2, 0, 1)).astype(F32)
        if ok is not None:
            b = jnp.where(ok[None], b, NEG)
        tiles.append(jnp.repeat(b, groups_per_head, axis=0).reshape(-1, PAGE))
    return jnp.stack(tiles, axis=0)


def _masked_queries(q, n_groups, width, scale):
    lane = np.arange(GROUP_W)[None, :] // width == np.arange(n_groups)[:, None]
    m = jnp.asarray(lane, F32)[None, :, None, :] * (q * scale)[:, None, :, :]
    return m.reshape(q.shape[0], n_groups * DEC_Q, GROUP_W).astype(BF16)


def _pad_new(a):
    return jnp.pad(a, ((0, 0), (0, PAGE - DEC_Q), (0, 0))).astype(BF16)


SEL_CHUNK = 32


def _dsa_sel_body(pt_ref, wqi_ref, wcol_ref, kinew_ref, tri_ref, *rest, n_pages, n_pad, n_sel):
    del pt_ref
    ki_refs = rest[:PPS]
    sel_ref, key_ref = rest[PPS:]
    g = pl.program_id(1)
    wqi = wqi_ref[...]
    wcol = wcol_ref[...]

    def score(kb):
        s = jnp.maximum(_dot_nt(wqi, kb), 0.0) * wcol
        out = s[0:DEC_Q]
        for e in range(1, IDX_HEADS):
            out = out + s[e * DEC_Q:(e + 1) * DEC_Q]
        return out

    for i in range(PPS):
        rows = pl.ds(pl.multiple_of((g * PPS + i) * DEC_Q, DEC_Q), DEC_Q)
        key_ref[rows, :] = _sort_key(score(ki_refs[i][...].astype(BF16)))

    @pl.when(g == n_pages // PPS - 1)
    def _():
        t = lax.broadcasted_iota(I32, (DEC_Q, PAGE), 0)
        c = lax.broadcasted_iota(I32, (DEC_Q, PAGE), 1)
        key_ref[n_pages * DEC_Q:(n_pages + 1) * DEC_Q, :] = jnp.where(c <= t, _sort_key(score(kinew_ref[...])), INT_MIN)
        if n_pad > n_pages + 1:
            key_ref[(n_pages + 1) * DEC_Q:, :] = jnp.full(((n_pad - n_pages - 1) * DEC_Q, PAGE), INT_MIN, I32)

        def count(pred):
            tot = jnp.zeros((DEC_Q, PAGE), F32)
            for ch in range(n_pad // SEL_CHUNK):
                hit = jnp.where(pred(key_ref[ch * SEL_CHUNK * DEC_Q:(ch + 1) * SEL_CHUNK * DEC_Q, :]), 1.0, 0.0)
                for k in range(SEL_CHUNK):
                    tot = tot + hit[k * DEC_Q:(k + 1) * DEC_Q]
            return jnp.sum(tot, axis=-1, keepdims=True)

        tile_rows = lambda x: jnp.tile(jnp.broadcast_to(x, (DEC_Q, PAGE)), (SEL_CHUNK, 1))
        thr = _kth_largest(lambda cand: count(lambda k, cb=tile_rows(cand): k >= cb), float(n_sel), (DEC_Q, 1))
        room = float(n_sel) - count(lambda k, tb=tile_rows(thr): k > tb)

        def select(j, seen):
            k = key_ref[pl.ds(pl.multiple_of(j * DEC_Q, DEC_Q), DEC_Q), :]
            eq = jnp.where(k == thr, 1.0, 0.0)
            rank = seen + _dot(eq.astype(BF16), tri_ref[...])
            keep = (k > thr) | ((k == thr) & (rank <= room))
            sel_ref[j] = jnp.where(keep, 0.0, NEG)
            return seen + jnp.sum(eq, axis=-1, keepdims=True)

        lax.fori_loop(0, n_pages + 1, select, jnp.zeros((DEC_Q, 1), F32))


def _dsa_decode_select(page_table, wqi, wcol, ki_new, cache_ki, l):
    nreq, n_pages = page_table.shape
    n_sel = min(TOPK_MAX, (n_pages * PAGE + DEC_Q) // 4)
    n_pad = -(-(n_pages + 1) // SEL_CHUNK) * SEL_CHUNK
    tri = jnp.asarray(np.arange(PAGE)[:, None] <= np.arange(PAGE)[None, :], BF16)
    rows = IDX_HEADS * DEC_Q
    page_spec = lambda i: pl.BlockSpec((None, None, PAGE, IDX_DIM),
                                       lambda r, g, pt, i=i: (l, pt[r * n_pages + g * PPS + i], 0, 0))
    return pl.pallas_call(
        functools.partial(_dsa_sel_body, n_pages=n_pages, n_pad=n_pad, n_sel=n_sel),
        grid_spec=pltpu.PrefetchScalarGridSpec(
            num_scalar_prefetch=1,
            grid=(nreq, n_pages // PPS),
            in_specs=[pl.BlockSpec((None, rows, IDX_DIM), lambda r, g, pt: (r, 0, 0)),
                      pl.BlockSpec((None, rows, PAGE), lambda r, g, pt: (r, 0, 0)),
                      pl.BlockSpec((None, PAGE, IDX_DIM), lambda r, g, pt: (r, 0, 0)),
                      pl.BlockSpec((PAGE, PAGE), lambda r, g, pt: (0, 0))]
            + [page_spec(i) for i in range(PPS)],
            out_specs=pl.BlockSpec((None, n_pages + 1, DEC_Q, PAGE), lambda r, g, pt: (r, 0, 0, 0)),
            scratch_shapes=[pltpu.VMEM((n_pad * DEC_Q, PAGE), I32)]),
        out_shape=jax.ShapeDtypeStruct((nreq, n_pages + 1, DEC_Q, PAGE), F32),
        compiler_params=_cparams(("parallel", "arbitrary")),
        name="dsa_select",
    )(page_table.reshape(-1), wqi, wcol, ki_new, tri, *([cache_ki] * PPS))


def _layer_params(l, prm):
    f32 = F32
    lb_all = jnp.cumsum(jax.nn.softmax(prm['hgrn_lb_logits'].astype(f32), axis=0), axis=0)
    lb = lb_all[l] - lb_all[0]
    lam_init = 0.8 - 0.6 * math.exp(-0.3 * l)
    lam = (jnp.exp(jnp.sum(prm['lq1'][l] * prm['lk1'][l])) - jnp.exp(jnp.sum(prm['lq2'][l] * prm['lk2'][l])) + lam_init)
    w_in = jnp.pad(prm['w_in'][l], ((0, 0), (0, D_IN_PAD - D_IN))).astype(BF16)
    return dict(
        w_in=w_in, log_lb=jnp.maximum(jnp.log(lb), NEG), log1m_lb=jnp.log1p(-lb), lam=lam.astype(f32), lam_init=lam_init,
        bias_b=_bias_tiles(prm['rel_bias'][:, :N_HEADS]), bias_d=_bias_tiles(prm['rel_bias'][:, N_HEADS:]),
        w_out=prm['w_out'][l].astype(BF16), w_gu=prm['w_gate_up'][l].astype(BF16), w_down=prm['w_down'][l].astype(BF16))


def _layer_prompt(x2d, bsz, seq, l, prm, lp, final):
    n = bsz * seq
    p = _proj(x2d, prm['attn_norm'][l], lp['w_in'])
    cols = lambda name: p[:, GROUP_W * COL[name]:GROUP_W * (COL[name] + 1)]
    b_k, b_v, d_k, d_v = cols('b_k'), cols('b_v'), cols('d_k'), cols('d_v')
    d_ki = p[:, 128 * TAIL_COL128:128 * TAIL_COL128 + IDX_DIM]

    st0 = jnp.zeros((bsz, GROUP_W, GROUP_W), F32)
    y_a, st = _hgrn(p, bsz, seq, lp['log_lb'], lp['log1m_lb'], prm['hgrn_norm'][l], st0, seq)

    y_b = _diff_prompt(p, bsz, seq, b_k.astype(BF16).reshape(bsz, seq, GROUP_W), _value_heads(b_v, bsz, seq),
                       lp['lam'], lp['bias_b'], prm['diff_norm'][l], 1.0 - lp['lam_init'])

    mix, bias_tile = _sgu_params(prm['sgu_w'][l], prm['sgu_b'][l], seq)
    y_c, c_v = _sgu(p, prm['sgu_norm'][l], mix, bias_tile)

    ki_rep = jnp.tile(d_ki, (1, IDX_HEADS)).astype(BF16).reshape(bsz, seq, GROUP_W)
    y_d = _dsa_prompt(p, bsz, seq, d_k.astype(BF16).reshape(bsz, seq, GROUP_W), _value_heads(d_v, bsz, seq),
                      ki_rep, lp['bias_d'])

    x_new = _post(x2d, (y_a, y_b, y_c, y_d), prm['mix_scale'][l], lp['w_out'], prm['ffn_norm'][l], lp['w_gu'],
                  lp['w_down'], prm['final_norm'], final)
    hd = lambda a: a.reshape(bsz, seq, N_HEADS, HEAD_DV)
    return x_new, (hd(b_k), hd(b_v), hd(d_k), hd(d_v), d_ki.reshape(bsz, seq, IDX_DIM), _blockdiag_to_state(st),
                   c_v.reshape(bsz, seq, GROUP_W))


def _layer_decode(x2d, nreq, l, prm, lp, past, final):
    n = nreq * DEC_Q
    pt = past['page_table']
    past_len = pt.shape[1] * PAGE
    p = _proj(x2d, prm['attn_norm'][l], lp['w_in'])
    cols = lambda name: p[:, GROUP_W * COL[name]:GROUP_W * (COL[name] + 1)]
    per_req = lambda a: a.reshape(nreq, DEC_Q, a.shape[-1])
    b_k, b_v, d_k, d_v = cols('b_k'), cols('b_v'), cols('d_k'), cols('d_v')
    tail = p[:, 128 * TAIL_COL128:128 * (TAIL_COL128 + 1)]
    d_ki = tail[:, :IDX_DIM]

    p16 = jnp.pad(per_req(p), ((0, 0), (0, HG - DEC_Q), (0, 0))).reshape(nreq * HG, D_IN_PAD)
    y_a16, st = _hgrn(p16, nreq, HG, lp['log_lb'], lp['log1m_lb'], prm['hgrn_norm'][l],
                      _state_to_blockdiag(past['hgrn'][l]), DEC_Q)
    y_a = y_a16.reshape(nreq, HG, GROUP_W)[:, :DEC_Q].reshape(n, GROUP_W)

    mix, bias_tile = _sgu_params(prm['sgu_w'][l], prm['sgu_b'][l], DEC_Q)
    y_c, c_v = _sgu(p, prm['sgu_norm'][l], mix, bias_tile)

    y_b = _paged_attn(pt, lp['lam'], _masked_queries(per_req(cols('b_q')), 2 * N_HEADS, DIFF_DQ, DIFF_DQ ** -0.5),
                      _decode_add_tiles(prm['rel_bias'][:, :N_HEADS], past_len, 2), _pad_new(per_req(b_k)),
                      _pad_new(per_req(b_v)), jnp.tile(prm['diff_norm'][l], N_HEADS), None,
                      past['k_diff'], past['v_diff'], l, diff=True, out_scale=1.0 - lp['lam_init'])

    qi = per_req(cols('d_qi')).reshape(nreq, DEC_Q, IDX_HEADS, IDX_DIM)
    wqi = jnp.swapaxes(qi, 1, 2).reshape(nreq, IDX_HEADS * DEC_Q, IDX_DIM).astype(BF16)
    w = per_req(tail[:, IDX_DIM:IDX_DIM + IDX_HEADS] * (IDX_HEADS ** -0.5 * IDX_DIM ** -0.5))
    wcol = jnp.broadcast_to(jnp.swapaxes(w, 1, 2).reshape(nreq, IDX_HEADS * DEC_Q, 1), (nreq, IDX_HEADS * DEC_Q, PAGE))
    ki_new = jnp.pad(per_req(d_ki), ((0, 0), (0, PAGE - DEC_Q), (0, 0))).astype(BF16)
    sel = _dsa_decode_select(pt, wqi, wcol, ki_new, past['k_index'], l)
    y_d = _paged_attn(pt, lp['lam'], _masked_queries(per_req(cols('d_q')), N_HEADS, HEAD_DV, HEAD_DV ** -0.5),
                      _decode_add_tiles(prm['rel_bias'][:, N_HEADS:], past_len, 1), _pad_new(per_req(d_k)),
                      _pad_new(per_req(d_v)), jnp.ones((GROUP_W,), F32), sel,
                      past['k_sparse'], past['v_sparse'], l, diff=False, out_scale=1.0)

    x_new = _post(x2d, (y_a, y_b, y_c, y_d), prm['mix_scale'][l], lp['w_out'], prm['ffn_norm'][l], lp['w_gu'],
                  lp['w_down'], prm['final_norm'], final)
    hd = lambda a: a.reshape(nreq, DEC_Q, N_HEADS, HEAD_DV)
    return x_new, (hd(b_k), hd(b_v), hd(d_k), hd(d_v), d_ki.reshape(nreq, DEC_Q, IDX_DIM), _blockdiag_to_state(st),
                   c_v.reshape(nreq, DEC_Q, GROUP_W))


def kernel(x_prompt, x_sample, cache_k_diff, cache_v_diff, cache_k_sparse, cache_v_sparse, cache_k_index, state_hgrn, page_table, w_in, w_out, attn_norm, ffn_norm, final_norm, w_gate_up, w_down, hgrn_lb_logits, hgrn_norm, diff_lambda_q1, diff_lambda_k1, diff_lambda_q2, diff_lambda_k2, diff_norm, sgu_w, sgu_b, sgu_norm, mix_scale, rel_bias):
    prm = dict(w_in=w_in, w_out=w_out, attn_norm=attn_norm, ffn_norm=ffn_norm, final_norm=final_norm,
               w_gate_up=w_gate_up, w_down=w_down, hgrn_lb_logits=hgrn_lb_logits, hgrn_norm=hgrn_norm,
               lq1=diff_lambda_q1, lk1=diff_lambda_k1, lq2=diff_lambda_q2, lk2=diff_lambda_k2, diff_norm=diff_norm,
               sgu_w=sgu_w, sgu_b=sgu_b, sgu_norm=sgu_norm, mix_scale=mix_scale, rel_bias=rel_bias)
    past = dict(page_table=page_table, k_diff=cache_k_diff, v_diff=cache_v_diff, k_sparse=cache_k_sparse,
                v_sparse=cache_v_sparse, k_index=cache_k_index, hgrn=state_hgrn)
    depth = w_in.shape[0]
    bsz, seq, d = x_prompt.shape
    nreq = x_sample.shape[0]
    xp = x_prompt.reshape(bsz * seq, d)
    xs = x_sample.reshape(nreq * DEC_Q, d)
    st_p, st_s = [], []
    for l in range(depth):
        lp = _layer_params(l, prm)
        final = l == depth - 1
        xp, sp = _layer_prompt(xp, bsz, seq, l, prm, lp, final)
        xs, ss = _layer_decode(xs, nreq, l, prm, lp, past, final)
        st_p.append(sp)
        st_s.append(ss)
    stk = lambda lst, i: jnp.stack([s[i] for s in lst], axis=0)
    return (xp.reshape(bsz, seq, d), xs.reshape(nreq, DEC_Q, d),
            stk(st_p, 0), stk(st_p, 1), stk(st_p, 2), stk(st_p, 3), stk(st_p, 4), stk(st_p, 5),
            stk(st_s, 0), stk(st_s, 1), stk(st_s, 2), stk(st_s, 3), stk(st_s, 4), stk(st_s, 5), stk(st_s, 6))
```

```python
import functools
import math

import numpy as np
import jax
import jax.numpy as jnp
from jax import lax
from jax.experimental import pallas as pl
from jax.experimental.pallas import tpu as pltpu

F32 = jnp.float32
BF16 = jnp.bfloat16
I32 = jnp.int32

D_MODEL = 1024
GROUP_W = 256
N_HEADS = 4
HEAD_DV = 64
DIFF_DQ = 32
IDX_HEADS = 8
IDX_DIM = 32
TOPK_MAX = 256
REL_BUCKETS = 32
REL_EXACT = 16
REL_MAX_DIST = 128
PAGE = 128
D_FF = 2816
EPS = 1e-6
D_IN = 3368
D_IN_PAD = 3456
COL = {'a_q': 0, 'a_f': 1, 'a_i': 2, 'a_g': 3, 'b_q': 4, 'b_k': 5, 'b_v': 6, 'c_u': 7, 'c_v': 8,
       'd_q': 9, 'd_k': 10, 'd_v': 11, 'd_qi': 12}
TAIL_COL128 = 26
NEG = -1e30
INT_MIN = -2 ** 31
TQ = 128
HG = 16
VMEM_LIMIT = 56 * 1024 * 1024


def _cparams(sem):
    return pltpu.CompilerParams(dimension_semantics=sem, vmem_limit_bytes=VMEM_LIMIT)


def _dot(a, b):
    return jnp.dot(a, b, preferred_element_type=F32)


def _dot_nt(a, b):
    return lax.dot_general(a, b, (((1,), (1,)), ((), ())), preferred_element_type=F32)


def _dot_tn(a, b):
    return lax.dot_general(a, b, (((0,), (0,)), ((), ())), preferred_element_type=F32)


def _sigmoid(x):
    return 1.0 / (1.0 + jnp.exp(-x))


def _split3(x):
    hi = x.astype(BF16)
    r = x - hi.astype(F32)
    mid = r.astype(BF16)
    lo = (r - mid.astype(F32)).astype(BF16)
    return hi, mid, lo


def _dot_exact_lhs(m_bf16, x):
    hi, mid, lo = _split3(x)
    return _dot(m_bf16, hi) + _dot(m_bf16, mid) + _dot(m_bf16, lo)


def _dot_exact_rhs(x, m_bf16):
    hi, mid, lo = _split3(x)
    return _dot(hi, m_bf16) + _dot(mid, m_bf16) + _dot(lo, m_bf16)


def _proj_body(x_ref, g_ref, w_ref, o_ref):
    x = x_ref[...]
    h = x * lax.rsqrt(jnp.mean(x * x, axis=-1, keepdims=True) + EPS) * g_ref[...]
    o_ref[...] = _dot(h.astype(BF16), w_ref[...])


def _proj(x2d, gain, w_bf16):
    n, d = x2d.shape
    dn = w_bf16.shape[1]
    tm = min(256, n)
    return pl.pallas_call(
        _proj_body,
        grid=(n // tm,),
        in_specs=[pl.BlockSpec((tm, d), lambda i: (i, 0)),
                  pl.BlockSpec((1, d), lambda i: (0, 0)),
                  pl.BlockSpec((d, dn), lambda i: (0, 0))],
        out_specs=pl.BlockSpec((tm, dn), lambda i: (i, 0)),
        out_shape=jax.ShapeDtypeStruct((n, dn), F32),
        compiler_params=_cparams(("parallel",)),
        name="proj",
    )(x2d, gain.reshape(1, d), w_bf16)


FF_CHUNK = 256


def _outproj_body(x_ref, ya_ref, yb_ref, yc_ref, yd_ref, ms_ref, wo_ref, o_ref):
    acc = x_ref[...]
    for i, y_ref in enumerate((ya_ref, yb_ref, yc_ref, yd_ref)):
        lo, hi = GROUP_W * i, GROUP_W * (i + 1)
        yi = (y_ref[...] * ms_ref[:, lo:hi]).astype(BF16)
        acc = acc + _dot(yi, wo_ref[lo:hi, :])
    o_ref[...] = acc


def _ffn_body(x_ref, fn_ref, wgu_ref, wd_ref, fin_ref, o_ref, acc_ref, *, final):
    x = x_ref[...]
    acc_ref[...] = x
    h = (x * lax.rsqrt(jnp.mean(x * x, axis=-1, keepdims=True) + EPS) * fn_ref[...]).astype(BF16)
    for c in range(D_FF // FF_CHUNK):
        lo, hi = FF_CHUNK * c, FF_CHUNK * (c + 1)
        g = _dot(h, wgu_ref[:, lo:hi])
        u = _dot(h, wgu_ref[:, D_FF + lo:D_FF + hi])
        a = (g * _sigmoid(g) * u).astype(BF16)
        acc_ref[...] += _dot(a, wd_ref[lo:hi, :])
    out = acc_ref[...]
    if final:
        out = out * lax.rsqrt(jnp.mean(out * out, axis=-1, keepdims=True) + EPS) * fin_ref[...]
    o_ref[...] = out


def _post(x2d, ys, mix_scale, wo, ffn_norm, wgu, wd, final_norm, final):
    n, d = x2d.shape
    tm = min(256, n)
    row = lambda i: (i, 0)
    const = lambda i: (0, 0)
    x1 = pl.pallas_call(
        _outproj_body,
        grid=(n // tm,),
        in_specs=[pl.BlockSpec((tm, d), row)]
        + [pl.BlockSpec((tm, GROUP_W), row) for _ in range(4)]
        + [pl.BlockSpec((1, d), const), pl.BlockSpec((d, d), const)],
        out_specs=pl.BlockSpec((tm, d), row),
        out_shape=jax.ShapeDtypeStruct((n, d), F32),
        compiler_params=_cparams(("parallel",)),
        name="outproj",
    )(x2d, *ys, mix_scale.reshape(1, d), wo)
    return pl.pallas_call(
        functools.partial(_ffn_body, final=final),
        grid=(n // tm,),
        in_specs=[pl.BlockSpec((tm, d), row),
                  pl.BlockSpec((1, d), const),
                  pl.BlockSpec((d, 2 * D_FF), const),
                  pl.BlockSpec((D_FF, d), const),
                  pl.BlockSpec((1, d), const)],
        out_specs=pl.BlockSpec((tm, d), row),
        out_shape=jax.ShapeDtypeStruct((n, d), F32),
        scratch_shapes=[pltpu.VMEM((tm, d), F32)],
        compiler_params=_cparams(("parallel",)),
        name="ffn",
    )(x1, ffn_norm.reshape(1, d), wgu, wd, final_norm.reshape(1, d))


def _hgrn_consts(tr):
    r = np.arange(tr)
    same = (r[:, None] // HG) == (r[None, :] // HG)
    tri = same & (r[None, :] <= r[:, None])
    c = np.arange(GROUP_W)
    heads = (c[:, None] // HEAD_DV) == (c[None, :] // HEAD_DV)
    return (jnp.asarray(tri, BF16), jnp.asarray(same, BF16), jnp.asarray(heads, BF16), jnp.asarray(heads, F32))


def _shift_rows(x, d, row):
    if d == 0:
        return x
    return jnp.where(row >= d, pltpu.roll(x, d, axis=0), 0.0)


def _hgrn_body(aq_ref, af_ref, ai_ref, ag_ref, la_ref, l1_ref, hn_ref, tri_ref, same_ref, e_ref, mask_ref, st0_ref,
               ya_ref, st_ref, cum_ref, tot_ref, kk_ref, qq_ref, qe_ref, ke_ref, o_ref, *, seq, tr, valid_rows):
    st_ref[...] = st0_ref[...]

    def prep(t, carry):
        r0 = pl.multiple_of(t * tr, tr)
        rows = pl.ds(r0, tr)
        z = af_ref[rows, :]
        ls = jnp.minimum(z, 0.0) - jnp.log(1.0 + jnp.exp(-jnp.abs(z)))
        b = l1_ref[...] + ls
        a = la_ref[...]
        lf = jnp.maximum(a, b) + jnp.log(1.0 + jnp.exp(-jnp.abs(a - b)))
        kk = 1.0 - jnp.exp(lf)
        if valid_rows < seq:
            ok = (lax.broadcasted_iota(I32, (tr, GROUP_W), 0) + r0) < valid_rows
            lf = jnp.where(ok, lf, 0.0)
            kk = jnp.where(ok, kk, 0.0)
        cum = _dot_exact_lhs(tri_ref[...], lf)
        tot = _dot_exact_lhs(same_ref[...], lf)
        q = aq_ref[rows, :]
        qq = q * _sigmoid(q)
        cum_ref[rows, :] = cum
        tot_ref[rows, :] = tot
        kk_ref[rows, :] = kk
        qq_ref[rows, :] = qq
        qe_ref[rows, :] = (qq * jnp.exp(cum)).astype(BF16)
        ke_ref[rows, :] = (kk * jnp.exp(tot - cum)).astype(BF16)
        return carry

    lax.fori_loop(0, seq // tr, prep, 0)

    row = lax.broadcasted_iota(I32, (HG, GROUP_W), 0)

    def block(i, carry):
        rows = pl.ds(pl.multiple_of(i * HG, HG), HG)
        cum = cum_ref[rows, :]
        kk = kk_ref[rows, :]
        qq = qq_ref[rows, :]
        v = ai_ref[rows, :]
        o = _dot_nt(qe_ref[rows, :], st_ref[...].astype(BF16))
        for d in range(HG):
            x = qq * _shift_rows(kk, d, row) * jnp.exp(cum - _shift_rows(cum, d, row))
            o = o + _dot(x.astype(BF16), e_ref[...]) * _shift_rows(v, d, row)
        o_ref[rows, :] = o
        dec = jnp.exp(tot_ref[pl.ds(pl.multiple_of(i * HG, HG), 1), :])
        upd = _dot_tn(v.astype(BF16), ke_ref[rows, :])
        st_ref[...] = st_ref[...] * dec + mask_ref[...] * upd
        return carry

    lax.fori_loop(0, seq // HG, block, 0)

    def fin(t, carry):
        rows = pl.ds(pl.multiple_of(t * tr, tr), tr)
        o = o_ref[rows, :]
        hi = (o * o).astype(BF16)
        lo = (o * o - hi.astype(F32)).astype(BF16)
        msq = (_dot(hi, e_ref[...]) + _dot(lo, e_ref[...])) * (1.0 / HEAD_DV)
        g = ag_ref[rows, :]
        ya_ref[rows, :] = o * lax.rsqrt(msq + EPS) * hn_ref[...] * (g * _sigmoid(g))
        return carry

    lax.fori_loop(0, seq // tr, fin, 0)


def _hgrn(p2d, bsz, seq, log_lb, log1m_lb, hnorm, st0, valid_rows):
    tr = min(128, seq)
    tri, same, e_bf, mask = _hgrn_consts(tr)
    col = lambda name: pl.BlockSpec((seq, GROUP_W), lambda b, c=COL[name]: (b, c))
    vec = pl.BlockSpec((1, GROUP_W), lambda b: (0, 0))
    mat = lambda n: pl.BlockSpec((n, n), lambda b: (0, 0))
    st_spec = pl.BlockSpec((None, GROUP_W, GROUP_W), lambda b: (b, 0, 0))
    big = lambda dt: pltpu.VMEM((seq, GROUP_W), dt)
    return pl.pallas_call(
        functools.partial(_hgrn_body, seq=seq, tr=tr, valid_rows=valid_rows),
        grid=(bsz,),
        in_specs=[col('a_q'), col('a_f'), col('a_i'), col('a_g'), vec, vec, vec, mat(tr), mat(tr),
                  mat(GROUP_W), mat(GROUP_W), st_spec],
        out_specs=[pl.BlockSpec((seq, GROUP_W), lambda b: (b, 0)), st_spec],
        out_shape=[jax.ShapeDtypeStruct((bsz * seq, GROUP_W), F32),
                   jax.ShapeDtypeStruct((bsz, GROUP_W, GROUP_W), F32)],
        scratch_shapes=[big(F32), big(F32), big(F32), big(F32), big(BF16), big(BF16), big(F32)],
        compiler_params=_cparams(("parallel",)),
        name="hgrn",
    )(p2d, p2d, p2d, p2d, log_lb.reshape(1, GROUP_W), log1m_lb.reshape(1, GROUP_W), hnorm.reshape(1, GROUP_W),
      tri, same, e_bf, mask, st0)


def _state_to_blockdiag(s):
    st = jnp.swapaxes(s, 2, 3)
    eye = jnp.eye(N_HEADS, dtype=s.dtype)
    return jnp.einsum('bhvk,hg->bhvgk', st, eye).reshape(s.shape[0], GROUP_W, GROUP_W)


def _blockdiag_to_state(st):
    b = st.shape[0]
    st5 = st.reshape(b, N_HEADS, HEAD_DV, N_HEADS, HEAD_DV)
    diag = jnp.stack([st5[:, h, :, h, :] for h in range(N_HEADS)], axis=1)
    return jnp.swapaxes(diag, 2, 3)


def _gelu(x):
    return 0.5 * x * (1.0 + lax.erf(x * (2.0 ** -0.5)))


def _sgu_body(cu_ref, cv_ref, sn_ref, mix_ref, bias_ref, yc_ref, cvn_ref, *, tr):
    u = _gelu(cu_ref[...])
    v = _gelu(cv_ref[...])
    vn = v * lax.rsqrt(jnp.mean(v * v, axis=-1, keepdims=True) + EPS) * sn_ref[...]
    cvn_ref[...] = vn
    vb = vn.astype(BF16)
    lane = lax.broadcasted_iota(I32, (tr, GROUP_W), 1)
    mixed = bias_ref[...]
    for g in range(N_HEADS):
        mg = _dot(mix_ref[g], vb)
        mixed = mixed + jnp.where((lane >= HEAD_DV * g) & (lane < HEAD_DV * (g + 1)), mg, 0.0)
    yc_ref[...] = u * mixed


def _sgu(p2d, snorm, mix_bf16, bias_tile):
    n = p2d.shape[0]
    tr = 128
    col = lambda name: pl.BlockSpec((tr, GROUP_W), lambda i, c=COL[name]: (i, c))
    out = pl.BlockSpec((tr, GROUP_W), lambda i: (i, 0))
    return pl.pallas_call(
        functools.partial(_sgu_body, tr=tr),
        grid=(n // tr,),
        in_specs=[col('c_u'), col('c_v'),
                  pl.BlockSpec((1, GROUP_W), lambda i: (0, 0)),
                  pl.BlockSpec((N_HEADS, tr, tr), lambda i: (0, 0, 0)),
                  pl.BlockSpec((tr, GROUP_W), lambda i: (0, 0))],
        out_specs=[out, out],
        out_shape=[jax.ShapeDtypeStruct((n, GROUP_W), F32)] * 2,
        compiler_params=_cparams(("parallel",)),
        name="sgu",
    )(p2d, p2d, snorm.reshape(1, GROUP_W), mix_bf16, bias_tile)


def _sgu_params(w, b, seq):
    c = min(128, seq)
    wt = jnp.tril(w[:, :c, :c])
    mix = jnp.einsum('rs,gtu->grtsu', jnp.eye(128 // c, dtype=w.dtype), wt).reshape(N_HEADS, 128, 128)
    bt = jnp.tile(jnp.transpose(b[:, :c]), (128 // c, 1))
    return mix.astype(BF16), jnp.repeat(bt, HEAD_DV, axis=1)


def _rel_bucket(rel):
    n = jnp.maximum(rel, 0)
    nf = jnp.maximum(n, 1).astype(F32)
    large = REL_EXACT + (jnp.log(nf / REL_EXACT) / math.log(REL_MAX_DIST / REL_EXACT)
                         * (REL_BUCKETS - REL_EXACT)).astype(I32)
    large = jnp.minimum(large, REL_BUCKETS - 1)
    return jnp.where(n < REL_EXACT, n, large)


def _bias_tiles(tab):
    assert REL_MAX_DIST <= TQ
    q = np.arange(TQ)[:, None]
    k = np.arange(TQ)[None, :]
    tiles = []
    for back in range(3):
        rel = jnp.asarray(q - k + TQ * back, I32)
        t = jnp.transpose(tab[_rel_bucket(rel)], (2, 0, 1)).astype(F32)
        if back == 0:
            t = jnp.where(jnp.asarray(k <= q)[None], t, NEG)
        tiles.append(t)
    return jnp.stack(tiles, axis=0)


def _lane_band(x, lo, width):
    lane = lax.broadcasted_iota(I32, x.shape, x.ndim - 1)
    return jnp.where((lane >= lo) & (lane < lo + width), x, 0.0)


def _softmax_step(s, m_ref, acc_ref, idx, v):
    m_old = m_ref[idx]
    m_new = jnp.maximum(m_old, jnp.max(s, axis=-1, keepdims=True))
    alpha = jnp.exp(m_old - m_new)
    p = jnp.exp(s - m_new)
    acc_ref[idx] = alpha * acc_ref[idx] + _dot(p.astype(BF16), v)
    m_ref[idx] = m_new


def _normalised(acc):
    return acc / pltpu.roll(acc, HEAD_DV, axis=1)


def _pack_heads(o_ref, res):
    lane = lax.broadcasted_iota(I32, res[0].shape, 1)
    for pair in range(2):
        both = jnp.where(lane < HEAD_DV, res[2 * pair], pltpu.roll(res[2 * pair + 1], HEAD_DV, axis=1))
        o_ref[:, 2 * HEAD_DV * pair:2 * HEAD_DV * (pair + 1)] = both


def _diff_body(lam_ref, q_ref, k_ref, v_ref, bias_ref, gain_ref, o_ref, qm_ref, m_ref, acc_ref, *, out_scale):
    i = pl.program_id(1)
    q = q_ref[...] * (DIFF_DQ ** -0.5)
    for hc in range(2 * N_HEADS):
        qm_ref[hc * TQ:(hc + 1) * TQ, :] = _lane_band(q, DIFF_DQ * hc, DIFF_DQ).astype(BF16)

    def scores(j):
        rows = pl.ds(pl.multiple_of(j * TQ, TQ), TQ)
        return _dot_nt(qm_ref[...], k_ref[rows, :]) + bias_ref[jnp.minimum(i - j, 2)], rows

    m_ref[...] = jnp.full(m_ref.shape, NEG, F32)

    def lane_max(j, carry):
        m_ref[...] = jnp.maximum(m_ref[...], scores(j)[0])
        return carry

    lax.fori_loop(0, i + 1, lane_max, 0)
    m_ref[...] = jnp.broadcast_to(jnp.max(m_ref[...], axis=-1, keepdims=True), m_ref.shape)
    acc_ref[...] = jnp.zeros(acc_ref.shape, F32)

    def attend(j, carry):
        s, rows = scores(j)
        p = jnp.exp(s - m_ref[...]).astype(BF16)
        for h in range(N_HEADS):
            blk = slice(2 * h * TQ, (2 * h + 2) * TQ)
            acc_ref[blk, :] += _dot(p[blk], v_ref[h, rows, :])
        return carry

    lax.fori_loop(0, i + 1, attend, 0)

    lam = lam_ref[0]
    lane = lax.broadcasted_iota(I32, (TQ, 2 * HEAD_DV), 1)
    res = []
    for h in range(N_HEADS):
        o = (_normalised(acc_ref[2 * h * TQ:(2 * h + 1) * TQ, :])
             - lam * _normalised(acc_ref[(2 * h + 1) * TQ:(2 * h + 2) * TQ, :]))
        msq = jnp.sum(jnp.where(lane < HEAD_DV, o * o, 0.0), axis=-1, keepdims=True) * (1.0 / HEAD_DV)
        res.append(o * lax.rsqrt(msq + EPS) * gain_ref[...] * out_scale)
    _pack_heads(o_ref, res)


def _value_heads(v2d, bsz, seq):
    v = jnp.transpose(v2d.reshape(bsz, seq, N_HEADS, HEAD_DV), (0, 2, 1, 3))
    return jnp.concatenate([v, jnp.ones_like(v)], axis=-1).astype(BF16)


def _diff_prompt(p2d, bsz, seq, k_bf16, v_heads, lam, bias, gain, out_scale):
    nq = seq // TQ
    return pl.pallas_call(
        functools.partial(_diff_body, out_scale=out_scale),
        grid=(bsz, nq),
        in_specs=[pl.BlockSpec(memory_space=pltpu.SMEM),
                  pl.BlockSpec((TQ, GROUP_W), lambda b, i: (b * nq + i, COL['b_q'])),
                  pl.BlockSpec((None, seq, GROUP_W), lambda b, i: (b, 0, 0)),
                  pl.BlockSpec((None, N_HEADS, seq, 2 * HEAD_DV), lambda b, i: (b, 0, 0, 0)),
                  pl.BlockSpec((3, 2 * N_HEADS * TQ, TQ), lambda b, i: (0, 0, 0)),
                  pl.BlockSpec((1, 2 * HEAD_DV), lambda b, i: (0, 0))],
        out_specs=pl.BlockSpec((TQ, GROUP_W), lambda b, i: (b * nq + i, 0)),
        out_shape=jax.ShapeDtypeStruct((bsz * seq, GROUP_W), F32),
        scratch_shapes=[pltpu.VMEM((2 * N_HEADS * TQ, GROUP_W), BF16),
                        pltpu.VMEM((2 * N_HEADS * TQ, 2 * HEAD_DV), F32),
                        pltpu.VMEM((2 * N_HEADS * TQ, 2 * HEAD_DV), F32)],
        compiler_params=_cparams(("parallel", "parallel")),
        name="diff_prompt",
    )(lam.reshape(1), p2d, k_bf16, v_heads, jnp.repeat(bias, 2, axis=1).reshape(3, 2 * N_HEADS * TQ, TQ),
      jnp.tile(gain, 2).reshape(1, 2 * HEAD_DV))


def _sort_key(score):
    bits = lax.bitcast_convert_type(score, I32)
    return jnp.where(bits < 0, bits ^ 0x7FFFFFFF, bits)


def _kth_largest(count_ge, n_sel, shape):
    t = jnp.full(shape, INT_MIN, I32)
    for bit in range(31, -1, -1):
        step = INT_MIN if bit == 31 else (1 << bit)
        cand = t + jnp.int32(step)
        t = jnp.where(count_ge(cand) >= n_sel, cand, t)
    return t


def _dsa_body(q_ref, qi_ref, wt_ref, k_ref, vt_ref, ki_ref, bias_ref, tri_ref, o_ref,
              qm_ref, qim_ref, key_ref, sel_ref, m_ref, acc_ref, *, n_sel):
    i = pl.program_id(1)
    q = q_ref[...] * (HEAD_DV ** -0.5)
    for h in range(N_HEADS):
        qm_ref[h * TQ:(h + 1) * TQ, :] = _lane_band(q, HEAD_DV * h, HEAD_DV).astype(BF16)
    qi = qi_ref[...]
    for e in range(IDX_HEADS):
        qim_ref[e * TQ:(e + 1) * TQ, :] = _lane_band(qi, IDX_DIM * e, IDX_DIM).astype(BF16)
    w = wt_ref[...] * (IDX_HEADS ** -0.5 * IDX_DIM ** -0.5)
    krow = lax.broadcasted_iota(I32, (TQ, TQ), 0)
    qcol = lax.broadcasted_iota(I32, (TQ, TQ), 1)

    def score(j, carry):
        s = jnp.maximum(_dot_nt(ki_ref[pl.ds(pl.multiple_of(j * TQ, TQ), TQ), :], qim_ref[...]), 0.0)
        sc = s[:, 0:TQ] * w[0:1, :]
        for e in range(1, IDX_HEADS):
            sc = sc + s[:, e * TQ:(e + 1) * TQ] * w[e:e + 1, :]
        causal = (krow + j * TQ) <= (qcol + i * TQ)
        key_ref[j] = jnp.where(causal, _sort_key(sc), INT_MIN)
        return carry

    lax.fori_loop(0, i + 1, score, 0)

    def count(pred):
        def body(j, c):
            return c + jnp.where(pred(key_ref[j]), 1.0, 0.0)
        return jnp.sum(lax.fori_loop(0, i + 1, body, jnp.zeros((TQ, TQ), F32)), axis=0, keepdims=True)

    thr = _kth_largest(lambda cand: count(lambda k: k >= cand), float(n_sel), (1, TQ))
    room = float(n_sel) - count(lambda k: k > thr)

    def select(j, seen):
        k = key_ref[j]
        eq = jnp.where(k == thr, 1.0, 0.0)
        rank = seen + _dot(tri_ref[...], eq.astype(BF16))
        keep = (k > thr) | ((k == thr) & (rank <= room))
        sel_ref[j] = jnp.where(keep, 0.0, NEG)
        return seen + jnp.sum(eq, axis=0, keepdims=True)

    lax.fori_loop(0, i + 1, select, jnp.zeros((1, TQ), F32))

    m_ref[...] = jnp.full(m_ref.shape, NEG, F32)
    acc_ref[...] = jnp.zeros(acc_ref.shape, F32)

    def attend(j, carry):
        sel = sel_ref[j]
        s = (_dot_nt(k_ref[pl.ds(pl.multiple_of(j * TQ, TQ), TQ), :], qm_ref[...]) + bias_ref[jnp.minimum(i - j, 2)]
             + jnp.concatenate([sel] * N_HEADS, axis=1))
        m_old = m_ref[...]
        m_new = jnp.maximum(m_old, jnp.max(s, axis=0, keepdims=True))
        alpha = jnp.exp(m_old - m_new)
        p = jnp.exp(s - m_new).astype(BF16)
        for h in range(N_HEADS):
            cols = slice(h * TQ, (h + 1) * TQ)
            acc_ref[h] = alpha[:, cols] * acc_ref[h] + _dot(vt_ref[j, h], p[:, cols])
        m_ref[...] = m_new
        return carry

    lax.fori_loop(0, i + 1, attend, 0)
    out_t = jnp.concatenate([acc_ref[h][:HEAD_DV, :] / acc_ref[h][HEAD_DV:HEAD_DV + 1, :] for h in range(N_HEADS)],
                            axis=0)
    o_ref[...] = jnp.transpose(out_t)


def _value_blocks_t(v2d, bsz, seq):
    v = jnp.transpose(v2d.reshape(bsz, seq // TQ, TQ, N_HEADS, HEAD_DV), (0, 1, 3, 4, 2))
    return jnp.concatenate([v, jnp.ones_like(v)], axis=3).astype(BF16)


def _dsa_prompt(p2d, bsz, seq, k_bf16, v_blocks_t, ki_rep, w_t, bias):
    nq = seq // TQ
    n_sel = min(TOPK_MAX, seq // 4)
    tri = jnp.asarray(np.arange(TQ)[None, :] <= np.arange(TQ)[:, None], BF16)
    bias_t = jnp.transpose(bias, (0, 3, 1, 2)).reshape(3, TQ, N_HEADS * TQ)
    qcol = lambda c: pl.BlockSpec((TQ, GROUP_W), lambda b, i, c=c: (b * nq + i, c))
    full = pl.BlockSpec((None, seq, GROUP_W), lambda b, i: (b, 0, 0))
    return pl.pallas_call(
        functools.partial(_dsa_body, n_sel=n_sel),
        grid=(bsz, nq),
        in_specs=[qcol(COL['d_q']), qcol(COL['d_qi']),
                  pl.BlockSpec((IDX_HEADS, TQ), lambda b, i: (0, b * nq + i)),
                  full,
                  pl.BlockSpec((None, nq, N_HEADS, 2 * HEAD_DV, TQ), lambda b, i: (b, 0, 0, 0, 0)),
                  full,
                  pl.BlockSpec((3, TQ, N_HEADS * TQ), lambda b, i: (0, 0, 0)),
                  pl.BlockSpec((TQ, TQ), lambda b, i: (0, 0))],
        out_specs=pl.BlockSpec((TQ, GROUP_W), lambda b, i: (b * nq + i, 0)),
        out_shape=jax.ShapeDtypeStruct((bsz * seq, GROUP_W), F32),
        scratch_shapes=[pltpu.VMEM((N_HEADS * TQ, GROUP_W), BF16),
                        pltpu.VMEM((IDX_HEADS * TQ, GROUP_W), BF16),
                        pltpu.VMEM((nq, TQ, TQ), I32),
                        pltpu.VMEM((nq, TQ, TQ), F32),
                        pltpu.VMEM((1, N_HEADS * TQ), F32),
                        pltpu.VMEM((N_HEADS, 2 * HEAD_DV, TQ), F32)],
        compiler_params=_cparams(("parallel", "parallel")),
        name="dsa_prompt",
    )(p2d, p2d, w_t, k_bf16, v_blocks_t, ki_rep, bias_t, tri)


DEC_Q = 8
PPS = 8


def _paged_update(s, vts, m_ref, l_ref, acc_ref):
    n = len(vts)
    m_old = m_ref[...]
    m_new = jnp.maximum(m_old, jnp.max(s, axis=-1, keepdims=True))
    alpha = jnp.exp(m_old - m_new)
    p = jnp.exp(s - jnp.concatenate([m_new] * n, axis=1))
    l = alpha * l_ref[...]
    acc = jnp.concatenate([alpha, alpha], axis=1) * acc_ref[...]
    for i in range(n):
        pi = p[:, i * PAGE:(i + 1) * PAGE]
        l = l + pi
        acc = acc + _dot_nt(pi.astype(BF16), vts[i])
    l_ref[...] = l
    acc_ref[...] = acc
    m_ref[...] = m_new


def _paged_body(pt_ref, lam_ref, wq_ref, add_ref, knew_ref, vnew_ref, gain_ref, e_ref, *rest,
                n_pages, rows, diff, use_sel, out_scale):
    del pt_ref
    if use_sel:
        sel_ref, selnew_ref = rest[:2]
        rest = rest[2:]
    k_refs, v_refs = rest[:PPS], rest[PPS:2 * PPS]
    o_ref, m_ref, l_ref, acc_ref = rest[2 * PPS:]
    g = pl.program_id(1)

    @pl.when(g == 0)
    def _():
        m_ref[...] = jnp.full(m_ref.shape, NEG, F32)
        l_ref[...] = jnp.zeros(l_ref.shape, F32)
        acc_ref[...] = jnp.zeros(acc_ref.shape, F32)

    wq = wq_ref[...]
    last_step = g == n_pages // PPS - 1
    far = add_ref[0]
    parts = []
    for i in range(PPS):
        si = _dot(wq, k_refs[i][...].astype(BF16))
        si = si + (far if i < PPS - 1 else jnp.where(last_step, add_ref[1], far))
        if use_sel:
            si = si + jnp.tile(sel_ref[i], (rows // DEC_Q, 1))
        parts.append(si)
    _paged_update(jnp.concatenate(parts, axis=1), [v_refs[i][...].astype(BF16) for i in range(PPS)],
                  m_ref, l_ref, acc_ref)

    @pl.when(last_step)
    def _():
        s = _dot(wq, knew_ref[...]) + add_ref[2]
        if use_sel:
            s = s + jnp.tile(selnew_ref[0], (rows // DEC_Q, 1))
        _paged_update(s, [vnew_ref[...]], m_ref, l_ref, acc_ref)
        o_all = acc_ref[...] / jnp.sum(l_ref[...], axis=-1, keepdims=True)
        out = jnp.zeros((DEC_Q, GROUP_W), F32)
        for h in range(N_HEADS):
            if diff:
                lo = 2 * h * DEC_Q
                o_h = o_all[lo:lo + DEC_Q] - lam_ref[0] * o_all[lo + DEC_Q:lo + 2 * DEC_Q]
            else:
                o_h = o_all[h * DEC_Q:(h + 1) * DEC_Q]
            out = out + _lane_band(o_h, HEAD_DV * h, HEAD_DV)
        if diff:
            sq = out * out
            hi = sq.astype(BF16)
            lo_ = (sq - hi.astype(F32)).astype(BF16)
            msq = (_dot(hi, e_ref[...]) + _dot(lo_, e_ref[...])) * (1.0 / HEAD_DV)
            out = out * lax.rsqrt(msq + EPS) * gain_ref[...] * out_scale
        o_ref[...] = out


def _paged_attn(page_table, lam, wq, add, k_new, v_new, gain256, sel, cache_k, cache_v, l, *, diff, out_scale):
    nreq, n_pages = page_table.shape
    rows = wq.shape[1]
    n_pool = cache_k.shape[1]
    ck = _feature_major_pages(cache_k)
    cv = _feature_major_pages(cache_v)
    use_sel = sel is not None
    page_spec = lambda i: pl.BlockSpec((None, None, GROUP_W, PAGE),
                                       lambda r, g, pt, i=i: (l, pt[r * n_pages + g * PPS + i], 0, 0))
    per_req = lambda shape: pl.BlockSpec((None,) + shape, lambda r, g, pt: (r,) + (0,) * len(shape))
    const = lambda shape: pl.BlockSpec(shape, lambda r, g, pt: (0,) * len(shape))
    in_specs = [pl.BlockSpec(memory_space=pltpu.SMEM), per_req((rows, GROUP_W)), const((3, rows, PAGE)),
                per_req((GROUP_W, PAGE)), per_req((GROUP_W, PAGE)), const((1, GROUP_W)), const((GROUP_W, GROUP_W))]
    args = [lam.reshape(1), wq, add, k_new, v_new, gain256.reshape(1, GROUP_W), _hgrn_consts(128)[2]]
    if use_sel:
        in_specs += [pl.BlockSpec((None, PPS, DEC_Q, PAGE), lambda r, g, pt: (r, g, 0, 0)),
                     pl.BlockSpec((None, 1, DEC_Q, PAGE), lambda r, g, pt: (r, n_pages, 0, 0))]
        args += [sel, sel]
    in_specs += [page_spec(i) for i in range(PPS)] * 2
    args += [ck] * PPS + [cv] * PPS
    return pl.pallas_call(
        functools.partial(_paged_body, n_pages=n_pages, rows=rows, diff=diff, use_sel=use_sel, out_scale=out_scale),
        grid_spec=pltpu.PrefetchScalarGridSpec(
            num_scalar_prefetch=1,
            grid=(nreq, n_pages // PPS),
            in_specs=in_specs,
            out_specs=pl.BlockSpec((DEC_Q, GROUP_W), lambda r, g, pt: (r, 0)),
            scratch_shapes=[pltpu.VMEM((rows, PAGE), F32), pltpu.VMEM((rows, PAGE), F32),
                            pltpu.VMEM((rows, GROUP_W), F32)]),
        out_shape=jax.ShapeDtypeStruct((nreq * DEC_Q, GROUP_W), F32),
        compiler_params=_cparams(("parallel", "arbitrary")),
        name="paged_diff" if diff else "paged_dsa",
    )(page_table.reshape(-1), *args)


def _decode_add_tiles(tab, past, groups_per_head):
    t = np.arange(DEC_Q)[:, None]
    c = np.arange(PAGE)[None, :]
    far = jnp.asarray(np.broadcast_to(past + t - (past - PAGE - 1), (DEC_Q, PAGE)), I32)
    last = jnp.asarray(t + PAGE - c, I32)
    new = jnp.asarray(t - c, I32)
    ok_new = jnp.asarray((c <= t) & (c < DEC_Q))
    tiles = []
    for rel, ok in ((far, None), (last, None), (new, ok_new)):
        b = jnp.transpose(tab[_rel_bucket(rel)], (2, 0, 1)).astype(F32)
        if ok is not None:
            b = jnp.where(ok[None], b, NEG)
        tiles.append(jnp.repeat(b, groups_per_head, axis=0).reshape(-1, PAGE))
    return jnp.stack(tiles, axis=0)


def _masked_queries(q, n_groups, width, scale):
    lane = np.arange(GROUP_W)[None, :] // width == np.arange(n_groups)[:, None]
    m = jnp.asarray(lane, F32)[None, :, None, :] * (q * scale)[:, None, :, :]
    return m.reshape(q.shape[0], n_groups * DEC_Q, GROUP_W).astype(BF16)


def _pad_new(a):
    return jnp.pad(jnp.swapaxes(a, 1, 2), ((0, 0), (0, 0), (0, PAGE - DEC_Q))).astype(BF16)


def _feature_major_pages(cache):
    d, n_pool = cache.shape[:2]
    return jnp.swapaxes(cache.reshape(d, n_pool, PAGE, -1), 2, 3)


SEL_CHUNK = 32


def _dsa_sel_body(pt_ref, wqi_ref, wcol_ref, kinew_ref, tri_ref, *rest, n_pages, n_pad, n_sel):
    del pt_ref
    ki_refs = rest[:PPS]
    sel_ref, key_ref = rest[PPS:]
    g = pl.program_id(1)
    wqi = wqi_ref[...]
    wcol = wcol_ref[...]

    def score(kts):
        kt = kts[0] if len(kts) == 1 else jnp.concatenate(kts, axis=1)
        s = jnp.maximum(_dot(wqi, kt), 0.0) * jnp.concatenate([wcol] * len(kts), axis=1)
        out = s[0:DEC_Q]
        for e in range(1, IDX_HEADS):
            out = out + s[e * DEC_Q:(e + 1) * DEC_Q]
        return out

    keys = _sort_key(score([ki_refs[i][...].astype(BF16) for i in range(PPS)]))
    for i in range(PPS):
        rows = pl.ds(pl.multiple_of((g * PPS + i) * DEC_Q, DEC_Q), DEC_Q)
        key_ref[rows, :] = keys[:, i * PAGE:(i + 1) * PAGE]

    @pl.when(g == n_pages // PPS - 1)
    def _():
        t = lax.broadcasted_iota(I32, (DEC_Q, PAGE), 0)
        c = lax.broadcasted_iota(I32, (DEC_Q, PAGE), 1)
        key_ref[n_pages * DEC_Q:(n_pages + 1) * DEC_Q, :] = jnp.where(c <= t, _sort_key(score([kinew_ref[...]])), INT_MIN)
        if n_pad > n_pages + 1:
            key_ref[(n_pages + 1) * DEC_Q:, :] = jnp.full(((n_pad - n_pages - 1) * DEC_Q, PAGE), INT_MIN, I32)

        def count(pred):
            tot = jnp.zeros((DEC_Q, PAGE), F32)
            for ch in range(n_pad // SEL_CHUNK):
                hit = jnp.where(pred(key_ref[ch * SEL_CHUNK * DEC_Q:(ch + 1) * SEL_CHUNK * DEC_Q, :]), 1.0, 0.0)
                for k in range(SEL_CHUNK):
                    tot = tot + hit[k * DEC_Q:(k + 1) * DEC_Q]
            return jnp.sum(tot, axis=-1, keepdims=True)

        tile_rows = lambda x: jnp.tile(jnp.broadcast_to(x, (DEC_Q, PAGE)), (SEL_CHUNK, 1))
        thr = _kth_largest(lambda cand: count(lambda k, cb=tile_rows(cand): k >= cb), float(n_sel), (DEC_Q, 1))
        room = float(n_sel) - count(lambda k, tb=tile_rows(thr): k > tb)

        def select(j, seen):
            k = key_ref[pl.ds(pl.multiple_of(j * DEC_Q, DEC_Q), DEC_Q), :]
            eq = jnp.where(k == thr, 1.0, 0.0)
            rank = seen + _dot(eq.astype(BF16), tri_ref[...])
            keep = (k > thr) | ((k == thr) & (rank <= room))
            sel_ref[j] = jnp.where(keep, 0.0, NEG)
            return seen + jnp.sum(eq, axis=-1, keepdims=True)

        lax.fori_loop(0, n_pages + 1, select, jnp.zeros((DEC_Q, 1), F32))


def _dsa_decode_select(page_table, wqi, wcol, ki_new, cache_ki, l):
    nreq, n_pages = page_table.shape
    n_sel = min(TOPK_MAX, (n_pages * PAGE + DEC_Q) // 4)
    n_pad = -(-(n_pages + 1) // SEL_CHUNK) * SEL_CHUNK
    tri = jnp.asarray(np.arange(PAGE)[:, None] <= np.arange(PAGE)[None, :], BF16)
    rows = IDX_HEADS * DEC_Q
    page_spec = lambda i: pl.BlockSpec((None, None, IDX_DIM, PAGE),
                                       lambda r, g, pt, i=i: (l, pt[r * n_pages + g * PPS + i], 0, 0))
    cache_ki = _feature_major_pages(cache_ki)
    return pl.pallas_call(
        functools.partial(_dsa_sel_body, n_pages=n_pages, n_pad=n_pad, n_sel=n_sel),
        grid_spec=pltpu.PrefetchScalarGridSpec(
            num_scalar_prefetch=1,
            grid=(nreq, n_pages // PPS),
            in_specs=[pl.BlockSpec((None, rows, IDX_DIM), lambda r, g, pt: (r, 0, 0)),
                      pl.BlockSpec((None, rows, PAGE), lambda r, g, pt: (r, 0, 0)),
                      pl.BlockSpec((None, IDX_DIM, PAGE), lambda r, g, pt: (r, 0, 0)),
                      pl.BlockSpec((PAGE, PAGE), lambda r, g, pt: (0, 0))]
            + [page_spec(i) for i in range(PPS)],
            out_specs=pl.BlockSpec((None, n_pages + 1, DEC_Q, PAGE), lambda r, g, pt: (r, 0, 0, 0)),
            scratch_shapes=[pltpu.VMEM((n_pad * DEC_Q, PAGE), I32)]),
        out_shape=jax.ShapeDtypeStruct((nreq, n_pages + 1, DEC_Q, PAGE), F32),
        compiler_params=_cparams(("parallel", "arbitrary")),
        name="dsa_select",
    )(page_table.reshape(-1), wqi, wcol, ki_new, tri, *([cache_ki] * PPS))


def _layer_params(l, prm):
    f32 = F32
    lb_all = jnp.cumsum(jax.nn.softmax(prm['hgrn_lb_logits'].astype(f32), axis=0), axis=0)
    lb = lb_all[l] - lb_all[0]
    lam_init = 0.8 - 0.6 * math.exp(-0.3 * l)
    lam = (jnp.exp(jnp.sum(prm['lq1'][l] * prm['lk1'][l])) - jnp.exp(jnp.sum(prm['lq2'][l] * prm['lk2'][l])) + lam_init)
    w_in = jnp.pad(prm['w_in'][l], ((0, 0), (0, D_IN_PAD - D_IN))).astype(BF16)
    return dict(
        w_in=w_in, log_lb=jnp.maximum(jnp.log(lb), NEG), log1m_lb=jnp.log1p(-lb), lam=lam.astype(f32), lam_init=lam_init,
        bias_b=_bias_tiles(prm['rel_bias'][:, :N_HEADS]), bias_d=_bias_tiles(prm['rel_bias'][:, N_HEADS:]),
        w_out=prm['w_out'][l].astype(BF16), w_gu=prm['w_gate_up'][l].astype(BF16), w_down=prm['w_down'][l].astype(BF16))


def _layer_prompt(x2d, bsz, seq, l, prm, lp, final):
    n = bsz * seq
    p = _proj(x2d, prm['attn_norm'][l], lp['w_in'])
    cols = lambda name: p[:, GROUP_W * COL[name]:GROUP_W * (COL[name] + 1)]
    b_k, b_v, d_k, d_v = cols('b_k'), cols('b_v'), cols('d_k'), cols('d_v')
    d_ki = p[:, 128 * TAIL_COL128:128 * TAIL_COL128 + IDX_DIM]

    st0 = jnp.zeros((bsz, GROUP_W, GROUP_W), F32)
    y_a, st = _hgrn(p, bsz, seq, lp['log_lb'], lp['log1m_lb'], prm['hgrn_norm'][l], st0, seq)

    y_b = _diff_prompt(p, bsz, seq, b_k.astype(BF16).reshape(bsz, seq, GROUP_W), _value_heads(b_v, bsz, seq),
                       lp['lam'], lp['bias_b'], prm['diff_norm'][l], 1.0 - lp['lam_init'])

    mix, bias_tile = _sgu_params(prm['sgu_w'][l], prm['sgu_b'][l], seq)
    y_c, c_v = _sgu(p, prm['sgu_norm'][l], mix, bias_tile)

    ki_rep = jnp.tile(d_ki, (1, IDX_HEADS)).astype(BF16).reshape(bsz, seq, GROUP_W)
    w_t = jnp.transpose(p[:, 128 * TAIL_COL128 + IDX_DIM:128 * TAIL_COL128 + IDX_DIM + IDX_HEADS])
    y_d = _dsa_prompt(p, bsz, seq, d_k.astype(BF16).reshape(bsz, seq, GROUP_W), _value_blocks_t(d_v, bsz, seq),
                      ki_rep, w_t, lp['bias_d'])

    x_new = _post(x2d, (y_a, y_b, y_c, y_d), prm['mix_scale'][l], lp['w_out'], prm['ffn_norm'][l], lp['w_gu'],
                  lp['w_down'], prm['final_norm'], final)
    hd = lambda a: a.reshape(bsz, seq, N_HEADS, HEAD_DV)
    return x_new, (hd(b_k), hd(b_v), hd(d_k), hd(d_v), d_ki.reshape(bsz, seq, IDX_DIM), _blockdiag_to_state(st),
                   c_v.reshape(bsz, seq, GROUP_W))


def _layer_decode(x2d, nreq, l, prm, lp, past, final):
    n = nreq * DEC_Q
    pt = past['page_table']
    past_len = pt.shape[1] * PAGE
    p = _proj(x2d, prm['attn_norm'][l], lp['w_in'])
    cols = lambda name: p[:, GROUP_W * COL[name]:GROUP_W * (COL[name] + 1)]
    per_req = lambda a: a.reshape(nreq, DEC_Q, a.shape[-1])
    b_k, b_v, d_k, d_v = cols('b_k'), cols('b_v'), cols('d_k'), cols('d_v')
    tail = p[:, 128 * TAIL_COL128:128 * (TAIL_COL128 + 1)]
    d_ki = tail[:, :IDX_DIM]

    p16 = jnp.pad(per_req(p), ((0, 0), (0, HG - DEC_Q), (0, 0))).reshape(nreq * HG, D_IN_PAD)
    y_a16, st = _hgrn(p16, nreq, HG, lp['log_lb'], lp['log1m_lb'], prm['hgrn_norm'][l],
                      _state_to_blockdiag(past['hgrn'][l]), DEC_Q)
    y_a = y_a16.reshape(nreq, HG, GROUP_W)[:, :DEC_Q].reshape(n, GROUP_W)

    mix, bias_tile = _sgu_params(prm['sgu_w'][l], prm['sgu_b'][l], DEC_Q)
    y_c, c_v = _sgu(p, prm['sgu_norm'][l], mix, bias_tile)

    y_b = _paged_attn(pt, lp['lam'], _masked_queries(per_req(cols('b_q')), 2 * N_HEADS, DIFF_DQ, DIFF_DQ ** -0.5),
                      _decode_add_tiles(prm['rel_bias'][:, :N_HEADS], past_len, 2), _pad_new(per_req(b_k)),
                      _pad_new(per_req(b_v)), jnp.tile(prm['diff_norm'][l], N_HEADS), None,
                      past['k_diff'], past['v_diff'], l, diff=True, out_scale=1.0 - lp['lam_init'])

    qi = per_req(cols('d_qi')).reshape(nreq, DEC_Q, IDX_HEADS, IDX_DIM)
    wqi = jnp.swapaxes(qi, 1, 2).reshape(nreq, IDX_HEADS * DEC_Q, IDX_DIM).astype(BF16)
    w = per_req(tail[:, IDX_DIM:IDX_DIM + IDX_HEADS] * (IDX_HEADS ** -0.5 * IDX_DIM ** -0.5))
    wcol = jnp.broadcast_to(jnp.swapaxes(w, 1, 2).reshape(nreq, IDX_HEADS * DEC_Q, 1), (nreq, IDX_HEADS * DEC_Q, PAGE))
    sel = _dsa_decode_select(pt, wqi, wcol, _pad_new(per_req(d_ki)), past['k_index'], l)
    y_d = _paged_attn(pt, lp['lam'], _masked_queries(per_req(cols('d_q')), N_HEADS, HEAD_DV, HEAD_DV ** -0.5),
                      _decode_add_tiles(prm['rel_bias'][:, N_HEADS:], past_len, 1), _pad_new(per_req(d_k)),
                      _pad_new(per_req(d_v)), jnp.ones((GROUP_W,), F32), sel,
                      past['k_sparse'], past['v_sparse'], l, diff=False, out_scale=1.0)

    x_new = _post(x2d, (y_a, y_b, y_c, y_d), prm['mix_scale'][l], lp['w_out'], prm['ffn_norm'][l], lp['w_gu'],
                  lp['w_down'], prm['final_norm'], final)
    hd = lambda a: a.reshape(nreq, DEC_Q, N_HEADS, HEAD_DV)
    return x_new, (hd(b_k), hd(b_v), hd(d_k), hd(d_v), d_ki.reshape(nreq, DEC_Q, IDX_DIM), _blockdiag_to_state(st),
                   c_v.reshape(nreq, DEC_Q, GROUP_W))


def kernel(x_prompt, x_sample, cache_k_diff, cache_v_diff, cache_k_sparse, cache_v_sparse, cache_k_index, state_hgrn, page_table, w_in, w_out, attn_norm, ffn_norm, final_norm, w_gate_up, w_down, hgrn_lb_logits, hgrn_norm, diff_lambda_q1, diff_lambda_k1, diff_lambda_q2, diff_lambda_k2, diff_norm, sgu_w, sgu_b, sgu_norm, mix_scale, rel_bias):
    prm = dict(w_in=w_in, w_out=w_out, attn_norm=attn_norm, ffn_norm=ffn_norm, final_norm=final_norm,
               w_gate_up=w_gate_up, w_down=w_down, hgrn_lb_logits=hgrn_lb_logits, hgrn_norm=hgrn_norm,
               lq1=diff_lambda_q1, lk1=diff_lambda_k1, lq2=diff_lambda_q2, lk2=diff_lambda_k2, diff_norm=diff_norm,
               sgu_w=sgu_w, sgu_b=sgu_b, sgu_norm=sgu_norm, mix_scale=mix_scale, rel_bias=rel_bias)
    past = dict(page_table=page_table, k_diff=cache_k_diff, v_diff=cache_v_diff, k_sparse=cache_k_sparse,
                v_sparse=cache_v_sparse, k_index=cache_k_index, hgrn=state_hgrn)
    depth = w_in.shape[0]
    bsz, seq, d = x_prompt.shape
    nreq = x_sample.shape[0]
    xp = x_prompt.reshape(bsz * seq, d)
    xs = x_sample.reshape(nreq * DEC_Q, d)
    st_p, st_s = [], []
    for l in range(depth):
        lp = _layer_params(l, prm)
        final = l == depth - 1
        xp, sp = _layer_prompt(xp, bsz, seq, l, prm, lp, final)
        xs, ss = _layer_decode(xs, nreq, l, prm, lp, past, final)
        st_p.append(sp)
        st_s.append(ss)
    stk = lambda lst, i: jnp.stack([s[i] for s in lst], axis=0)
    return (xp.reshape(bsz, seq, d), xs.reshape(nreq, DEC_Q, d),
            stk(st_p, 0), stk(st_p, 1), stk(st_p, 2), stk(st_p, 3), stk(st_p, 4), stk(st_p, 5),
            stk(st_s, 0), stk(st_s, 1), stk(st_s, 2), stk(st_s, 3), stk(st_s, 4), stk(st_s, 5), stk(st_s, 6))
```

```python
import functools
import math

import numpy as np
import jax
import jax.numpy as jnp
from jax import lax
from jax.experimental import pallas as pl
from jax.experimental.pallas import tpu as pltpu

F32 = jnp.float32
BF16 = jnp.bfloat16
I32 = jnp.int32

D_MODEL = 1024
GROUP_W = 256
N_HEADS = 4
HEAD_DV = 64
DIFF_DQ = 32
IDX_HEADS = 8
IDX_DIM = 32
TOPK_MAX = 256
REL_BUCKETS = 32
REL_EXACT = 16
REL_MAX_DIST = 128
PAGE = 128
D_FF = 2816
EPS = 1e-6
D_IN = 3368
D_IN_PAD = 3456
COL = {'a_q': 0, 'a_f': 1, 'a_i': 2, 'a_g': 3, 'b_q': 4, 'b_k': 5, 'b_v': 6, 'c_u': 7, 'c_v': 8,
       'd_q': 9, 'd_k': 10, 'd_v': 11, 'd_qi': 12}
TAIL_COL128 = 26
NEG = -1e30
INT_MIN = -2 ** 31
TQ = 128
HG = 16
VMEM_LIMIT = 56 * 1024 * 1024


def _cparams(sem):
    return pltpu.CompilerParams(dimension_semantics=sem, vmem_limit_bytes=VMEM_LIMIT)


def _dot(a, b):
    return jnp.dot(a, b, preferred_element_type=F32)


def _dot_nt(a, b):
    return lax.dot_general(a, b, (((1,), (1,)), ((), ())), preferred_element_type=F32)


def _dot_tn(a, b):
    return lax.dot_general(a, b, (((0,), (0,)), ((), ())), preferred_element_type=F32)


def _sigmoid(x):
    return 1.0 / (1.0 + jnp.exp(-x))


def _split3(x):
    hi = x.astype(BF16)
    r = x - hi.astype(F32)
    mid = r.astype(BF16)
    lo = (r - mid.astype(F32)).astype(BF16)
    return hi, mid, lo


def _dot_exact_lhs(m_bf16, x):
    hi, mid, lo = _split3(x)
    return _dot(m_bf16, hi) + _dot(m_bf16, mid) + _dot(m_bf16, lo)


def _dot_exact_rhs(x, m_bf16):
    hi, mid, lo = _split3(x)
    return _dot(hi, m_bf16) + _dot(mid, m_bf16) + _dot(lo, m_bf16)


def _proj_body(x_ref, g_ref, w_ref, o_ref):
    x = x_ref[...]
    h = x * lax.rsqrt(jnp.mean(x * x, axis=-1, keepdims=True) + EPS) * g_ref[...]
    o_ref[...] = _dot(h.astype(BF16), w_ref[...])


def _proj(x2d, gain, w_bf16):
    n, d = x2d.shape
    dn = w_bf16.shape[1]
    tm = min(256, n)
    return pl.pallas_call(
        _proj_body,
        grid=(n // tm,),
        in_specs=[pl.BlockSpec((tm, d), lambda i: (i, 0)),
                  pl.BlockSpec((1, d), lambda i: (0, 0)),
                  pl.BlockSpec((d, dn), lambda i: (0, 0))],
        out_specs=pl.BlockSpec((tm, dn), lambda i: (i, 0)),
        out_shape=jax.ShapeDtypeStruct((n, dn), F32),
        compiler_params=_cparams(("parallel",)),
        name="proj",
    )(x2d, gain.reshape(1, d), w_bf16)


FF_CHUNK = 256


def _outproj_body(x_ref, ya_ref, yb_ref, yc_ref, yd_ref, ms_ref, wo_ref, o_ref):
    acc = x_ref[...]
    for i, y_ref in enumerate((ya_ref, yb_ref, yc_ref, yd_ref)):
        lo, hi = GROUP_W * i, GROUP_W * (i + 1)
        yi = (y_ref[...] * ms_ref[:, lo:hi]).astype(BF16)
        acc = acc + _dot(yi, wo_ref[lo:hi, :])
    o_ref[...] = acc


def _ffn_body(x_ref, fn_ref, wgu_ref, wd_ref, fin_ref, o_ref, acc_ref, *, final):
    x = x_ref[...]
    acc_ref[...] = x
    h = (x * lax.rsqrt(jnp.mean(x * x, axis=-1, keepdims=True) + EPS) * fn_ref[...]).astype(BF16)
    for c in range(D_FF // FF_CHUNK):
        lo, hi = FF_CHUNK * c, FF_CHUNK * (c + 1)
        g = _dot(h, wgu_ref[:, lo:hi])
        u = _dot(h, wgu_ref[:, D_FF + lo:D_FF + hi])
        a = (g * _sigmoid(g) * u).astype(BF16)
        acc_ref[...] += _dot(a, wd_ref[lo:hi, :])
    out = acc_ref[...]
    if final:
        out = out * lax.rsqrt(jnp.mean(out * out, axis=-1, keepdims=True) + EPS) * fin_ref[...]
    o_ref[...] = out


def _post(x2d, ys, mix_scale, wo, ffn_norm, wgu, wd, final_norm, final):
    n, d = x2d.shape
    tm = min(256, n)
    row = lambda i: (i, 0)
    const = lambda i: (0, 0)
    x1 = pl.pallas_call(
        _outproj_body,
        grid=(n // tm,),
        in_specs=[pl.BlockSpec((tm, d), row)]
        + [pl.BlockSpec((tm, GROUP_W), row) for _ in range(4)]
        + [pl.BlockSpec((1, d), const), pl.BlockSpec((d, d), const)],
        out_specs=pl.BlockSpec((tm, d), row),
        out_shape=jax.ShapeDtypeStruct((n, d), F32),
        compiler_params=_cparams(("parallel",)),
        name="outproj",
    )(x2d, *ys, mix_scale.reshape(1, d), wo)
    return pl.pallas_call(
        functools.partial(_ffn_body, final=final),
        grid=(n // tm,),
        in_specs=[pl.BlockSpec((tm, d), row),
                  pl.BlockSpec((1, d), const),
                  pl.BlockSpec((d, 2 * D_FF), const),
                  pl.BlockSpec((D_FF, d), const),
                  pl.BlockSpec((1, d), const)],
        out_specs=pl.BlockSpec((tm, d), row),
        out_shape=jax.ShapeDtypeStruct((n, d), F32),
        scratch_shapes=[pltpu.VMEM((tm, d), F32)],
        compiler_params=_cparams(("parallel",)),
        name="ffn",
    )(x1, ffn_norm.reshape(1, d), wgu, wd, final_norm.reshape(1, d))


def _hgrn_consts(tr):
    r = np.arange(tr)
    same = (r[:, None] // HG) == (r[None, :] // HG)
    tri = same & (r[None, :] <= r[:, None])
    c = np.arange(GROUP_W)
    heads = (c[:, None] // HEAD_DV) == (c[None, :] // HEAD_DV)
    return (jnp.asarray(tri, BF16), jnp.asarray(same, BF16), jnp.asarray(heads, BF16), jnp.asarray(heads, F32))


def _shift_rows(x, d, row):
    if d == 0:
        return x
    return jnp.where(row >= d, pltpu.roll(x, d, axis=0), 0.0)


def _hgrn_body(aq_ref, af_ref, ai_ref, ag_ref, la_ref, l1_ref, hn_ref, tri_ref, same_ref, e_ref, mask_ref, st0_ref,
               ya_ref, st_ref, cum_ref, tot_ref, kk_ref, qq_ref, qe_ref, ke_ref, o_ref, *, seq, tr, valid_rows):
    st_ref[...] = st0_ref[...]

    def prep(t, carry):
        r0 = pl.multiple_of(t * tr, tr)
        rows = pl.ds(r0, tr)
        z = af_ref[rows, :]
        ls = jnp.minimum(z, 0.0) - jnp.log(1.0 + jnp.exp(-jnp.abs(z)))
        b = l1_ref[...] + ls
        a = la_ref[...]
        lf = jnp.maximum(a, b) + jnp.log(1.0 + jnp.exp(-jnp.abs(a - b)))
        kk = 1.0 - jnp.exp(lf)
        if valid_rows < seq:
            ok = (lax.broadcasted_iota(I32, (tr, GROUP_W), 0) + r0) < valid_rows
            lf = jnp.where(ok, lf, 0.0)
            kk = jnp.where(ok, kk, 0.0)
        cum = _dot_exact_lhs(tri_ref[...], lf)
        tot = _dot_exact_lhs(same_ref[...], lf)
        q = aq_ref[rows, :]
        qq = q * _sigmoid(q)
        cum_ref[rows, :] = cum
        tot_ref[rows, :] = tot
        kk_ref[rows, :] = kk
        qq_ref[rows, :] = qq
        qe_ref[rows, :] = (qq * jnp.exp(cum)).astype(BF16)
        ke_ref[rows, :] = (kk * jnp.exp(tot - cum)).astype(BF16)
        return carry

    lax.fori_loop(0, seq // tr, prep, 0)

    row = lax.broadcasted_iota(I32, (HG, GROUP_W), 0)

    def block(i, carry):
        rows = pl.ds(pl.multiple_of(i * HG, HG), HG)
        cum = cum_ref[rows, :]
        kk = kk_ref[rows, :]
        qq = qq_ref[rows, :]
        v = ai_ref[rows, :]
        o = _dot_nt(qe_ref[rows, :], st_ref[...].astype(BF16))
        for d in range(HG):
            x = qq * _shift_rows(kk, d, row) * jnp.exp(cum - _shift_rows(cum, d, row))
            o = o + _dot(x.astype(BF16), e_ref[...]) * _shift_rows(v, d, row)
        o_ref[rows, :] = o
        dec = jnp.exp(tot_ref[pl.ds(pl.multiple_of(i * HG, HG), 1), :])
        upd = _dot_tn(v.astype(BF16), ke_ref[rows, :])
        st_ref[...] = st_ref[...] * dec + mask_ref[...] * upd
        return carry

    lax.fori_loop(0, seq // HG, block, 0)

    def fin(t, carry):
        rows = pl.ds(pl.multiple_of(t * tr, tr), tr)
        o = o_ref[rows, :]
        hi = (o * o).astype(BF16)
        lo = (o * o - hi.astype(F32)).astype(BF16)
        msq = (_dot(hi, e_ref[...]) + _dot(lo, e_ref[...])) * (1.0 / HEAD_DV)
        g = ag_ref[rows, :]
        ya_ref[rows, :] = o * lax.rsqrt(msq + EPS) * hn_ref[...] * (g * _sigmoid(g))
        return carry

    lax.fori_loop(0, seq // tr, fin, 0)


def _hgrn(p2d, bsz, seq, log_lb, log1m_lb, hnorm, st0, valid_rows):
    tr = min(128, seq)
    tri, same, e_bf, mask = _hgrn_consts(tr)
    col = lambda name: pl.BlockSpec((seq, GROUP_W), lambda b, c=COL[name]: (b, c))
    vec = pl.BlockSpec((1, GROUP_W), lambda b: (0, 0))
    mat = lambda n: pl.BlockSpec((n, n), lambda b: (0, 0))
    st_spec = pl.BlockSpec((None, GROUP_W, GROUP_W), lambda b: (b, 0, 0))
    big = lambda dt: pltpu.VMEM((seq, GROUP_W), dt)
    return pl.pallas_call(
        functools.partial(_hgrn_body, seq=seq, tr=tr, valid_rows=valid_rows),
        grid=(bsz,),
        in_specs=[col('a_q'), col('a_f'), col('a_i'), col('a_g'), vec, vec, vec, mat(tr), mat(tr),
                  mat(GROUP_W), mat(GROUP_W), st_spec],
        out_specs=[pl.BlockSpec((seq, GROUP_W), lambda b: (b, 0)), st_spec],
        out_shape=[jax.ShapeDtypeStruct((bsz * seq, GROUP_W), F32),
                   jax.ShapeDtypeStruct((bsz, GROUP_W, GROUP_W), F32)],
        scratch_shapes=[big(F32), big(F32), big(F32), big(F32), big(BF16), big(BF16), big(F32)],
        compiler_params=_cparams(("parallel",)),
        name="hgrn",
    )(p2d, p2d, p2d, p2d, log_lb.reshape(1, GROUP_W), log1m_lb.reshape(1, GROUP_W), hnorm.reshape(1, GROUP_W),
      tri, same, e_bf, mask, st0)


def _state_to_blockdiag(s):
    st = jnp.swapaxes(s, 2, 3)
    eye = jnp.eye(N_HEADS, dtype=s.dtype)
    return jnp.einsum('bhvk,hg->bhvgk', st, eye).reshape(s.shape[0], GROUP_W, GROUP_W)


def _blockdiag_to_state(st):
    b = st.shape[0]
    st5 = st.reshape(b, N_HEADS, HEAD_DV, N_HEADS, HEAD_DV)
    diag = jnp.stack([st5[:, h, :, h, :] for h in range(N_HEADS)], axis=1)
    return jnp.swapaxes(diag, 2, 3)


def _gelu(x):
    return 0.5 * x * (1.0 + lax.erf(x * (2.0 ** -0.5)))


def _sgu_body(cu_ref, cv_ref, sn_ref, mix_ref, bias_ref, yc_ref, cvn_ref, *, tr):
    u = _gelu(cu_ref[...])
    v = _gelu(cv_ref[...])
    vn = v * lax.rsqrt(jnp.mean(v * v, axis=-1, keepdims=True) + EPS) * sn_ref[...]
    cvn_ref[...] = vn
    vb = vn.astype(BF16)
    lane = lax.broadcasted_iota(I32, (tr, GROUP_W), 1)
    mixed = bias_ref[...]
    for g in range(N_HEADS):
        mg = _dot(mix_ref[g], vb)
        mixed = mixed + jnp.where((lane >= HEAD_DV * g) & (lane < HEAD_DV * (g + 1)), mg, 0.0)
    yc_ref[...] = u * mixed


def _sgu(p2d, snorm, mix_bf16, bias_tile):
    n = p2d.shape[0]
    tr = 128
    col = lambda name: pl.BlockSpec((tr, GROUP_W), lambda i, c=COL[name]: (i, c))
    out = pl.BlockSpec((tr, GROUP_W), lambda i: (i, 0))
    return pl.pallas_call(
        functools.partial(_sgu_body, tr=tr),
        grid=(n // tr,),
        in_specs=[col('c_u'), col('c_v'),
                  pl.BlockSpec((1, GROUP_W), lambda i: (0, 0)),
                  pl.BlockSpec((N_HEADS, tr, tr), lambda i: (0, 0, 0)),
                  pl.BlockSpec((tr, GROUP_W), lambda i: (0, 0))],
        out_specs=[out, out],
        out_shape=[jax.ShapeDtypeStruct((n, GROUP_W), F32)] * 2,
        compiler_params=_cparams(("parallel",)),
        name="sgu",
    )(p2d, p2d, snorm.reshape(1, GROUP_W), mix_bf16, bias_tile)


def _sgu_params(w, b, seq):
    c = min(128, seq)
    wt = jnp.tril(w[:, :c, :c])
    mix = jnp.einsum('rs,gtu->grtsu', jnp.eye(128 // c, dtype=w.dtype), wt).reshape(N_HEADS, 128, 128)
    bt = jnp.tile(jnp.transpose(b[:, :c]), (128 // c, 1))
    return mix.astype(BF16), jnp.repeat(bt, HEAD_DV, axis=1)


def _rel_bucket(rel):
    n = jnp.maximum(rel, 0)
    nf = jnp.maximum(n, 1).astype(F32)
    large = REL_EXACT + (jnp.log(nf / REL_EXACT) / math.log(REL_MAX_DIST / REL_EXACT)
                         * (REL_BUCKETS - REL_EXACT)).astype(I32)
    large = jnp.minimum(large, REL_BUCKETS - 1)
    return jnp.where(n < REL_EXACT, n, large)


def _bias_tiles(tab):
    assert REL_MAX_DIST <= TQ
    q = np.arange(TQ)[:, None]
    k = np.arange(TQ)[None, :]
    tiles = []
    for back in range(3):
        rel = jnp.asarray(q - k + TQ * back, I32)
        t = jnp.transpose(tab[_rel_bucket(rel)], (2, 0, 1)).astype(F32)
        if back == 0:
            t = jnp.where(jnp.asarray(k <= q)[None], t, NEG)
        tiles.append(t)
    return jnp.stack(tiles, axis=0)


def _lane_band(x, lo, width):
    lane = lax.broadcasted_iota(I32, x.shape, x.ndim - 1)
    return jnp.where((lane >= lo) & (lane < lo + width), x, 0.0)


TK = 2 * TQ


def _value_blocks_t(v2d, bsz, seq, tk):
    v = jnp.transpose(v2d.reshape(bsz, seq // tk, tk, N_HEADS, HEAD_DV), (0, 1, 3, 4, 2))
    return jnp.concatenate([v, jnp.ones_like(v)], axis=3).astype(BF16)


def _bias_tiles_t(bias, maps):
    t = jnp.transpose(bias, (0, 3, 1, 2))[:, :, :, None, :]
    t = jnp.broadcast_to(t, (3, TQ, N_HEADS, maps, TQ)).reshape(3, TQ, N_HEADS * maps * TQ)
    return jnp.concatenate([t, jnp.full_like(t[:1], NEG)], axis=0)


def _step_bias(bias_ref, back):
    first = bias_ref[jnp.minimum(back, 2)]
    second = bias_ref[jnp.where(back < 1, 3, jnp.minimum(back - 1, 2))]
    return jnp.concatenate([first, second], axis=0)


def _diff_body(lam_ref, q_ref, k_ref, vt_ref, bias_ref, gain_ref, o_ref, qm_ref, m_ref, acc_ref, *, out_scale):
    i = pl.program_id(1)
    q = q_ref[...] * (DIFF_DQ ** -0.5)
    for hc in range(2 * N_HEADS):
        qm_ref[hc * TQ:(hc + 1) * TQ, :] = _lane_band(q, DIFF_DQ * hc, DIFF_DQ).astype(BF16)
    m_ref[...] = jnp.full(m_ref.shape, NEG, F32)
    acc_ref[...] = jnp.zeros(acc_ref.shape, F32)

    def step(t, carry):
        s = (_dot_nt(k_ref[pl.ds(pl.multiple_of(t * TK, TK), TK), :], qm_ref[...])
             + _step_bias(bias_ref, i - 2 * t))
        m_old = m_ref[...]
        m_new = jnp.maximum(m_old, jnp.max(s, axis=0, keepdims=True))
        alpha = jnp.exp(m_old - m_new)
        p = jnp.exp(s - m_new).astype(BF16)
        for h in range(N_HEADS):
            cols = slice(2 * h * TQ, (2 * h + 2) * TQ)
            acc_ref[h] = alpha[:, cols] * acc_ref[h] + _dot(vt_ref[t, h], p[:, cols])
        m_ref[...] = m_new
        return carry

    lax.fori_loop(0, (i + 2) // 2, step, 0)

    lam = lam_ref[0]
    res = []
    for h in range(N_HEADS):
        a = acc_ref[h]
        n = a[:HEAD_DV, :] / a[HEAD_DV:HEAD_DV + 1, :]
        o = n[:, :TQ] - lam * n[:, TQ:]
        msq = jnp.mean(o * o, axis=0, keepdims=True)
        res.append(o * lax.rsqrt(msq + EPS) * gain_ref[...] * out_scale)
    o_ref[...] = jnp.transpose(jnp.concatenate(res, axis=0))


def _diff_prompt(p2d, bsz, seq, k_bf16, v_blocks_t, lam, bias, gain, out_scale):
    nq = seq // TQ
    assert nq % 2 == 0
    ncol = 2 * N_HEADS * TQ
    return pl.pallas_call(
        functools.partial(_diff_body, out_scale=out_scale),
        grid=(bsz, nq),
        in_specs=[pl.BlockSpec(memory_space=pltpu.SMEM),
                  pl.BlockSpec((TQ, GROUP_W), lambda b, i: (b * nq + i, COL['b_q'])),
                  pl.BlockSpec((None, seq, GROUP_W), lambda b, i: (b, 0, 0)),
                  pl.BlockSpec((None, seq // TK, N_HEADS, 2 * HEAD_DV, TK), lambda b, i: (b, 0, 0, 0, 0)),
                  pl.BlockSpec((4, TQ, ncol), lambda b, i: (0, 0, 0)),
                  pl.BlockSpec((HEAD_DV, TQ), lambda b, i: (0, 0))],
        out_specs=pl.BlockSpec((TQ, GROUP_W), lambda b, i: (b * nq + i, 0)),
        out_shape=jax.ShapeDtypeStruct((bsz * seq, GROUP_W), F32),
        scratch_shapes=[pltpu.VMEM((ncol, GROUP_W), BF16),
                        pltpu.VMEM((1, ncol), F32),
                        pltpu.VMEM((N_HEADS, 2 * HEAD_DV, 2 * TQ), F32)],
        compiler_params=_cparams(("parallel", "parallel")),
        name="diff_prompt",
    )(lam.reshape(1), p2d, k_bf16, v_blocks_t, _bias_tiles_t(bias, 2),
      jnp.broadcast_to(gain[:, None], (HEAD_DV, TQ)))


def _sort_key(score):
    bits = lax.bitcast_convert_type(score, I32)
    return jnp.where(bits < 0, bits ^ 0x7FFFFFFF, bits)


def _kth_largest(count_ge, n_sel, shape):
    t = jnp.full(shape, INT_MIN, I32)
    for bit in range(31, -1, -1):
        step = INT_MIN if bit == 31 else (1 << bit)
        cand = t + jnp.int32(step)
        t = jnp.where(count_ge(cand) >= n_sel, cand, t)
    return t


def _dsa_body(q_ref, qi_ref, wt_ref, k_ref, vt_ref, ki_ref, bias_ref, tri_ref, o_ref,
              qm_ref, qim_ref, key_ref, sel_ref, m_ref, acc_ref, *, n_sel):
    i = pl.program_id(1)
    q = q_ref[...] * (HEAD_DV ** -0.5)
    for h in range(N_HEADS):
        qm_ref[h * TQ:(h + 1) * TQ, :] = _lane_band(q, HEAD_DV * h, HEAD_DV).astype(BF16)
    qi = qi_ref[...]
    for e in range(IDX_HEADS):
        qim_ref[e * TQ:(e + 1) * TQ, :] = _lane_band(qi, IDX_DIM * e, IDX_DIM).astype(BF16)
    w = wt_ref[...] * (IDX_HEADS ** -0.5 * IDX_DIM ** -0.5)
    krow = lax.broadcasted_iota(I32, (TK, TQ), 0)
    qcol = lax.broadcasted_iota(I32, (TK, TQ), 1)
    n_step = (i + 2) // 2

    def score(t, carry):
        s = jnp.maximum(_dot_nt(ki_ref[pl.ds(pl.multiple_of(t * TK, TK), TK), :], qim_ref[...]), 0.0)
        sc = s[:, 0:TQ] * w[0:1, :]
        for e in range(1, IDX_HEADS):
            sc = sc + s[:, e * TQ:(e + 1) * TQ] * w[e:e + 1, :]
        causal = (krow + t * TK) <= (qcol + i * TQ)
        key_ref[t] = jnp.where(causal, _sort_key(sc), INT_MIN)
        return carry

    lax.fori_loop(0, n_step, score, 0)

    def count(pred):
        def body(t, c):
            return c + jnp.where(pred(key_ref[t]), 1.0, 0.0)
        return jnp.sum(lax.fori_loop(0, n_step, body, jnp.zeros((TK, TQ), F32)), axis=0, keepdims=True)

    thr = _kth_largest(lambda cand: count(lambda k: k >= cand), float(n_sel), (1, TQ))
    room = float(n_sel) - count(lambda k: k > thr)

    def select(t, seen):
        k = key_ref[t]
        eq = jnp.where(k == thr, 1.0, 0.0)
        rank = seen + _dot(tri_ref[...], eq.astype(BF16))
        keep = ((k > thr) | ((k == thr) & (rank <= room))) & (k != INT_MIN)
        sel_ref[t] = jnp.where(keep, 0.0, NEG)
        return seen + jnp.sum(eq, axis=0, keepdims=True)

    lax.fori_loop(0, n_step, select, jnp.zeros((1, TQ), F32))

    m_ref[...] = jnp.full(m_ref.shape, NEG, F32)
    acc_ref[...] = jnp.zeros(acc_ref.shape, F32)

    def attend(t, carry):
        sel = sel_ref[t]
        s = (_dot_nt(k_ref[pl.ds(pl.multiple_of(t * TK, TK), TK), :], qm_ref[...]) + _step_bias(bias_ref, i - 2 * t)
             + jnp.concatenate([sel] * N_HEADS, axis=1))
        m_old = m_ref[...]
        m_new = jnp.maximum(m_old, jnp.max(s, axis=0, keepdims=True))
        alpha = jnp.exp(m_old - m_new)
        p = jnp.exp(s - m_new).astype(BF16)
        for h in range(N_HEADS):
            cols = slice(h * TQ, (h + 1) * TQ)
            acc_ref[h] = alpha[:, cols] * acc_ref[h] + _dot(vt_ref[t, h], p[:, cols])
        m_ref[...] = m_new
        return carry

    lax.fori_loop(0, n_step, attend, 0)
    out_t = jnp.concatenate([acc_ref[h][:HEAD_DV, :] / acc_ref[h][HEAD_DV:HEAD_DV + 1, :] for h in range(N_HEADS)],
                            axis=0)
    o_ref[...] = jnp.transpose(out_t)


def _dsa_prompt(p2d, bsz, seq, k_bf16, v_blocks_t, ki_rep, w_t, bias):
    nq = seq // TQ
    assert nq % 2 == 0
    n_sel = min(TOPK_MAX, seq // 4)
    tri = jnp.asarray(np.arange(TK)[None, :] <= np.arange(TK)[:, None], BF16)
    qcol = lambda c: pl.BlockSpec((TQ, GROUP_W), lambda b, i, c=c: (b * nq + i, c))
    full = pl.BlockSpec((None, seq, GROUP_W), lambda b, i: (b, 0, 0))
    return pl.pallas_call(
        functools.partial(_dsa_body, n_sel=n_sel),
        grid=(bsz, nq),
        in_specs=[qcol(COL['d_q']), qcol(COL['d_qi']),
                  pl.BlockSpec((IDX_HEADS, TQ), lambda b, i: (0, b * nq + i)),
                  full,
                  pl.BlockSpec((None, seq // TK, N_HEADS, 2 * HEAD_DV, TK), lambda b, i: (b, 0, 0, 0, 0)),
                  full,
                  pl.BlockSpec((4, TQ, N_HEADS * TQ), lambda b, i: (0, 0, 0)),
                  pl.BlockSpec((TK, TK), lambda b, i: (0, 0))],
        out_specs=pl.BlockSpec((TQ, GROUP_W), lambda b, i: (b * nq + i, 0)),
        out_shape=jax.ShapeDtypeStruct((bsz * seq, GROUP_W), F32),
        scratch_shapes=[pltpu.VMEM((N_HEADS * TQ, GROUP_W), BF16),
                        pltpu.VMEM((IDX_HEADS * TQ, GROUP_W), BF16),
                        pltpu.VMEM((seq // TK, TK, TQ), I32),
                        pltpu.VMEM((seq // TK, TK, TQ), F32),
                        pltpu.VMEM((1, N_HEADS * TQ), F32),
                        pltpu.VMEM((N_HEADS, 2 * HEAD_DV, TQ), F32)],
        compiler_params=_cparams(("parallel", "parallel")),
        name="dsa_prompt",
    )(p2d, p2d, w_t, k_bf16, v_blocks_t, ki_rep, _bias_tiles_t(bias, 1), tri)


DEC_Q = 8
PPS_ATTN = 16
PPS_SELECT = 32
PAGE_GROUP = 8


def _paged_update(s, vts, m_ref, l_ref, acc_ref):
    n = len(vts)
    m_old = m_ref[...]
    m_new = jnp.maximum(m_old, jnp.max(s, axis=-1, keepdims=True))
    alpha = jnp.exp(m_old - m_new)
    p = jnp.exp(s - jnp.concatenate([m_new] * n, axis=1))
    l = alpha * l_ref[...]
    acc = jnp.concatenate([alpha, alpha], axis=1) * acc_ref[...]
    for i in range(n):
        pi = p[:, i * PAGE:(i + 1) * PAGE]
        l = l + pi
        acc = acc + _dot_nt(pi.astype(BF16), vts[i])
    l_ref[...] = l
    acc_ref[...] = acc
    m_ref[...] = m_new


def _paged_body(pt_ref, lam_ref, wq_ref, add_ref, knew_ref, vnew_ref, gain_ref, e_ref, *rest,
                n_pages, pps, rows, diff, use_sel, out_scale):
    del pt_ref
    if use_sel:
        sel_ref, selnew_ref = rest[:2]
        rest = rest[2:]
    k_refs, v_refs = rest[:pps], rest[pps:2 * pps]
    o_ref, m_ref, l_ref, acc_ref = rest[2 * pps:]
    g = pl.program_id(1)

    @pl.when(g == 0)
    def _():
        m_ref[...] = jnp.full(m_ref.shape, NEG, F32)
        l_ref[...] = jnp.zeros(l_ref.shape, F32)
        acc_ref[...] = jnp.zeros(acc_ref.shape, F32)

    wq = wq_ref[...]
    last_step = g == n_pages // pps - 1
    far = add_ref[0]
    for lo in range(0, pps, PAGE_GROUP):
        parts = []
        for i in range(lo, min(lo + PAGE_GROUP, pps)):
            si = _dot(wq, k_refs[i][...].astype(BF16))
            si = si + (far if i < pps - 1 else jnp.where(last_step, add_ref[1], far))
            if use_sel:
                si = si + jnp.tile(sel_ref[i], (rows // DEC_Q, 1))
            parts.append(si)
        _paged_update(jnp.concatenate(parts, axis=1),
                      [v_refs[i][...].astype(BF16) for i in range(lo, min(lo + PAGE_GROUP, pps))],
                      m_ref, l_ref, acc_ref)

    @pl.when(last_step)
    def _():
        s = _dot(wq, knew_ref[...]) + add_ref[2]
        if use_sel:
            s = s + jnp.tile(selnew_ref[0], (rows // DEC_Q, 1))
        _paged_update(s, [vnew_ref[...]], m_ref, l_ref, acc_ref)
        o_all = acc_ref[...] / jnp.sum(l_ref[...], axis=-1, keepdims=True)
        out = jnp.zeros((DEC_Q, GROUP_W), F32)
        for h in range(N_HEADS):
            if diff:
                lo = 2 * h * DEC_Q
                o_h = o_all[lo:lo + DEC_Q] - lam_ref[0] * o_all[lo + DEC_Q:lo + 2 * DEC_Q]
            else:
                o_h = o_all[h * DEC_Q:(h + 1) * DEC_Q]
            out = out + _lane_band(o_h, HEAD_DV * h, HEAD_DV)
        if diff:
            sq = out * out
            hi = sq.astype(BF16)
            lo_ = (sq - hi.astype(F32)).astype(BF16)
            msq = (_dot(hi, e_ref[...]) + _dot(lo_, e_ref[...])) * (1.0 / HEAD_DV)
            out = out * lax.rsqrt(msq + EPS) * gain_ref[...] * out_scale
        o_ref[...] = out


def _paged_attn(page_table, lam, wq, add, k_new, v_new, gain256, sel, cache_k, cache_v, l, *, diff, out_scale):
    nreq, n_pages = page_table.shape
    rows = wq.shape[1]
    n_pool = cache_k.shape[1]
    ck = _feature_major_pages(cache_k)
    cv = _feature_major_pages(cache_v)
    use_sel = sel is not None
    pps = min(PPS_ATTN, n_pages)
    page_spec = lambda i: pl.BlockSpec((None, None, GROUP_W, PAGE),
                                       lambda r, g, pt, i=i: (l, pt[r * n_pages + g * pps + i], 0, 0))
    per_req = lambda shape: pl.BlockSpec((None,) + shape, lambda r, g, pt: (r,) + (0,) * len(shape))
    const = lambda shape: pl.BlockSpec(shape, lambda r, g, pt: (0,) * len(shape))
    in_specs = [pl.BlockSpec(memory_space=pltpu.SMEM), per_req((rows, GROUP_W)), const((3, rows, PAGE)),
                per_req((GROUP_W, PAGE)), per_req((GROUP_W, PAGE)), const((1, GROUP_W)), const((GROUP_W, GROUP_W))]
    args = [lam.reshape(1), wq, add, k_new, v_new, gain256.reshape(1, GROUP_W), _hgrn_consts(128)[2]]
    if use_sel:
        in_specs += [pl.BlockSpec((None, pps, DEC_Q, PAGE), lambda r, g, pt: (r, g, 0, 0)),
                     pl.BlockSpec((None, 1, DEC_Q, PAGE), lambda r, g, pt: (r, n_pages, 0, 0))]
        args += [sel, sel]
    in_specs += [page_spec(i) for i in range(pps)] * 2
    args += [ck] * pps + [cv] * pps
    return pl.pallas_call(
        functools.partial(_paged_body, n_pages=n_pages, pps=pps, rows=rows, diff=diff, use_sel=use_sel,
                          out_scale=out_scale),
        grid_spec=pltpu.PrefetchScalarGridSpec(
            num_scalar_prefetch=1,
            grid=(nreq, n_pages // pps),
            in_specs=in_specs,
            out_specs=pl.BlockSpec((DEC_Q, GROUP_W), lambda r, g, pt: (r, 0)),
            scratch_shapes=[pltpu.VMEM((rows, PAGE), F32), pltpu.VMEM((rows, PAGE), F32),
                            pltpu.VMEM((rows, GROUP_W), F32)]),
        out_shape=jax.ShapeDtypeStruct((nreq * DEC_Q, GROUP_W), F32),
        compiler_params=_cparams(("parallel", "arbitrary")),
        name="paged_diff" if diff else "paged_dsa",
    )(page_table.reshape(-1), *args)


def _decode_add_tiles(tab, past, groups_per_head):
    t = np.arange(DEC_Q)[:, None]
    c = np.arange(PAGE)[None, :]
    far = jnp.asarray(np.broadcast_to(past + t - (past - PAGE - 1), (DEC_Q, PAGE)), I32)
    last = jnp.asarray(t + PAGE - c, I32)
    new = jnp.asarray(t - c, I32)
    ok_new = jnp.asarray((c <= t) & (c < DEC_Q))
    tiles = []
    for rel, ok in ((far, None), (last, None), (new, ok_new)):
        b = jnp.transpose(tab[_rel_bucket(rel)], (2, 0, 1)).astype(F32)
        if ok is not None:
            b = jnp.where(ok[None], b, NEG)
        tiles.append(jnp.repeat(b, groups_per_head, axis=0).reshape(-1, PAGE))
    return jnp.stack(tiles, axis=0)


def _masked_queries(q, n_groups, width, scale):
    lane = np.arange(GROUP_W)[None, :] // width == np.arange(n_groups)[:, None]
    m = jnp.asarray(lane, F32)[None, :, None, :] * (q * scale)[:, None, :, :]
    return m.reshape(q.shape[0], n_groups * DEC_Q, GROUP_W).astype(BF16)


def _pad_new(a):
    return jnp.pad(jnp.swapaxes(a, 1, 2), ((0, 0), (0, 0), (0, PAGE - DEC_Q))).astype(BF16)


def _feature_major_pages(cache):
    d, n_pool = cache.shape[:2]
    return jnp.swapaxes(cache.reshape(d, n_pool, PAGE, -1), 2, 3)


SEL_CHUNK = 32


def _dsa_sel_body(pt_ref, wqi_ref, wcol_ref, kinew_ref, tri_ref, *rest, n_pages, pps, n_pad, n_sel):
    del pt_ref
    ki_refs = rest[:pps]
    sel_ref, key_ref = rest[pps:]
    g = pl.program_id(1)
    wqi = wqi_ref[...]
    wcol = wcol_ref[...]

    def score(kts):
        kt = kts[0] if len(kts) == 1 else jnp.concatenate(kts, axis=1)
        s = jnp.maximum(_dot(wqi, kt), 0.0) * jnp.concatenate([wcol] * len(kts), axis=1)
        out = s[0:DEC_Q]
        for e in range(1, IDX_HEADS):
            out = out + s[e * DEC_Q:(e + 1) * DEC_Q]
        return out

    keys = _sort_key(score([ki_refs[i][...].astype(BF16) for i in range(pps)]))
    for i in range(pps):
        rows = pl.ds(pl.multiple_of((g * pps + i) * DEC_Q, DEC_Q), DEC_Q)
        key_ref[rows, :] = keys[:, i * PAGE:(i + 1) * PAGE]

    @pl.when(g == n_pages // pps - 1)
    def _():
        t = lax.broadcasted_iota(I32, (DEC_Q, PAGE), 0)
        c = lax.broadcasted_iota(I32, (DEC_Q, PAGE), 1)
        key_ref[n_pages * DEC_Q:(n_pages + 1) * DEC_Q, :] = jnp.where(c <= t, _sort_key(score([kinew_ref[...]])), INT_MIN)
        if n_pad > n_pages + 1:
            key_ref[(n_pages + 1) * DEC_Q:, :] = jnp.full(((n_pad - n_pages - 1) * DEC_Q, PAGE), INT_MIN, I32)

        def count(pred):
            tot = jnp.zeros((DEC_Q, PAGE), F32)
            for ch in range(n_pad // SEL_CHUNK):
                hit = jnp.where(pred(key_ref[ch * SEL_CHUNK * DEC_Q:(ch + 1) * SEL_CHUNK * DEC_Q, :]), 1.0, 0.0)
                for k in range(SEL_CHUNK):
                    tot = tot + hit[k * DEC_Q:(k + 1) * DEC_Q]
            return jnp.sum(tot, axis=-1, keepdims=True)

        tile_rows = lambda x: jnp.tile(jnp.broadcast_to(x, (DEC_Q, PAGE)), (SEL_CHUNK, 1))
        thr = _kth_largest(lambda cand: count(lambda k, cb=tile_rows(cand): k >= cb), float(n_sel), (DEC_Q, 1))
        room = float(n_sel) - count(lambda k, tb=tile_rows(thr): k > tb)

        def select(j, seen):
            k = key_ref[pl.ds(pl.multiple_of(j * DEC_Q, DEC_Q), DEC_Q), :]
            eq = jnp.where(k == thr, 1.0, 0.0)
            rank = seen + _dot(eq.astype(BF16), tri_ref[...])
            keep = (k > thr) | ((k == thr) & (rank <= room))
            sel_ref[j] = jnp.where(keep, 0.0, NEG)
            return seen + jnp.sum(eq, axis=-1, keepdims=True)

        lax.fori_loop(0, n_pages + 1, select, jnp.zeros((DEC_Q, 1), F32))


def _dsa_decode_select(page_table, wqi, wcol, ki_new, cache_ki, l):
    nreq, n_pages = page_table.shape
    n_sel = min(TOPK_MAX, (n_pages * PAGE + DEC_Q) // 4)
    n_pad = -(-(n_pages + 1) // SEL_CHUNK) * SEL_CHUNK
    tri = jnp.asarray(np.arange(PAGE)[:, None] <= np.arange(PAGE)[None, :], BF16)
    rows = IDX_HEADS * DEC_Q
    pps = min(PPS_SELECT, n_pages)
    page_spec = lambda i: pl.BlockSpec((None, None, IDX_DIM, PAGE),
                                       lambda r, g, pt, i=i: (l, pt[r * n_pages + g * pps + i], 0, 0))
    cache_ki = _feature_major_pages(cache_ki)
    return pl.pallas_call(
        functools.partial(_dsa_sel_body, n_pages=n_pages, pps=pps, n_pad=n_pad, n_sel=n_sel),
        grid_spec=pltpu.PrefetchScalarGridSpec(
            num_scalar_prefetch=1,
            grid=(nreq, n_pages // pps),
            in_specs=[pl.BlockSpec((None, rows, IDX_DIM), lambda r, g, pt: (r, 0, 0)),
                      pl.BlockSpec((None, rows, PAGE), lambda r, g, pt: (r, 0, 0)),
                      pl.BlockSpec((None, IDX_DIM, PAGE), lambda r, g, pt: (r, 0, 0)),
                      pl.BlockSpec((PAGE, PAGE), lambda r, g, pt: (0, 0))]
            + [page_spec(i) for i in range(pps)],
            out_specs=pl.BlockSpec((None, n_pages + 1, DEC_Q, PAGE), lambda r, g, pt: (r, 0, 0, 0)),
            scratch_shapes=[pltpu.VMEM((n_pad * DEC_Q, PAGE), I32)]),
        out_shape=jax.ShapeDtypeStruct((nreq, n_pages + 1, DEC_Q, PAGE), F32),
        compiler_params=_cparams(("parallel", "arbitrary")),
        name="dsa_select",
    )(page_table.reshape(-1), wqi, wcol, ki_new, tri, *([cache_ki] * pps))


def _layer_params(l, prm):
    f32 = F32
    lb_all = jnp.cumsum(jax.nn.softmax(prm['hgrn_lb_logits'].astype(f32), axis=0), axis=0)
    lb = lb_all[l] - lb_all[0]
    lam_init = 0.8 - 0.6 * math.exp(-0.3 * l)
    lam = (jnp.exp(jnp.sum(prm['lq1'][l] * prm['lk1'][l])) - jnp.exp(jnp.sum(prm['lq2'][l] * prm['lk2'][l])) + lam_init)
    w_in = jnp.pad(prm['w_in'][l], ((0, 0), (0, D_IN_PAD - D_IN))).astype(BF16)
    return dict(
        w_in=w_in, log_lb=jnp.maximum(jnp.log(lb), NEG), log1m_lb=jnp.log1p(-lb), lam=lam.astype(f32), lam_init=lam_init,
        bias_b=_bias_tiles(prm['rel_bias'][:, :N_HEADS]), bias_d=_bias_tiles(prm['rel_bias'][:, N_HEADS:]),
        w_out=prm['w_out'][l].astype(BF16), w_gu=prm['w_gate_up'][l].astype(BF16), w_down=prm['w_down'][l].astype(BF16))


def _layer_prompt(x2d, bsz, seq, l, prm, lp, final):
    n = bsz * seq
    p = _proj(x2d, prm['attn_norm'][l], lp['w_in'])
    cols = lambda name: p[:, GROUP_W * COL[name]:GROUP_W * (COL[name] + 1)]
    b_k, b_v, d_k, d_v = cols('b_k'), cols('b_v'), cols('d_k'), cols('d_v')
    d_ki = p[:, 128 * TAIL_COL128:128 * TAIL_COL128 + IDX_DIM]

    st0 = jnp.zeros((bsz, GROUP_W, GROUP_W), F32)
    y_a, st = _hgrn(p, bsz, seq, lp['log_lb'], lp['log1m_lb'], prm['hgrn_norm'][l], st0, seq)

    y_b = _diff_prompt(p, bsz, seq, b_k.astype(BF16).reshape(bsz, seq, GROUP_W), _value_blocks_t(b_v, bsz, seq, TK),
                       lp['lam'], lp['bias_b'], prm['diff_norm'][l], 1.0 - lp['lam_init'])

    mix, bias_tile = _sgu_params(prm['sgu_w'][l], prm['sgu_b'][l], seq)
    y_c, c_v = _sgu(p, prm['sgu_norm'][l], mix, bias_tile)

    ki_rep = jnp.tile(d_ki, (1, IDX_HEADS)).astype(BF16).reshape(bsz, seq, GROUP_W)
    w_t = jnp.transpose(p[:, 128 * TAIL_COL128 + IDX_DIM:128 * TAIL_COL128 + IDX_DIM + IDX_HEADS])
    y_d = _dsa_prompt(p, bsz, seq, d_k.astype(BF16).reshape(bsz, seq, GROUP_W), _value_blocks_t(d_v, bsz, seq, TK),
                      ki_rep, w_t, lp['bias_d'])

    x_new = _post(x2d, (y_a, y_b, y_c, y_d), prm['mix_scale'][l], lp['w_out'], prm['ffn_norm'][l], lp['w_gu'],
                  lp['w_down'], prm['final_norm'], final)
    hd = lambda a: a.reshape(bsz, seq, N_HEADS, HEAD_DV)
    return x_new, (hd(b_k), hd(b_v), hd(d_k), hd(d_v), d_ki.reshape(bsz, seq, IDX_DIM), _blockdiag_to_state(st),
                   c_v.reshape(bsz, seq, GROUP_W))


def _layer_decode(x2d, nreq, l, prm, lp, past, final):
    n = nreq * DEC_Q
    pt = past['page_table']
    past_len = pt.shape[1] * PAGE
    p = _proj(x2d, prm['attn_norm'][l], lp['w_in'])
    cols = lambda name: p[:, GROUP_W * COL[name]:GROUP_W * (COL[name] + 1)]
    per_req = lambda a: a.reshape(nreq, DEC_Q, a.shape[-1])
    b_k, b_v, d_k, d_v = cols('b_k'), cols('b_v'), cols('d_k'), cols('d_v')
    tail = p[:, 128 * TAIL_COL128:128 * (TAIL_COL128 + 1)]
    d_ki = tail[:, :IDX_DIM]

    p16 = jnp.pad(per_req(p), ((0, 0), (0, HG - DEC_Q), (0, 0))).reshape(nreq * HG, D_IN_PAD)
    y_a16, st = _hgrn(p16, nreq, HG, lp['log_lb'], lp['log1m_lb'], prm['hgrn_norm'][l],
                      _state_to_blockdiag(past['hgrn'][l]), DEC_Q)
    y_a = y_a16.reshape(nreq, HG, GROUP_W)[:, :DEC_Q].reshape(n, GROUP_W)

    mix, bias_tile = _sgu_params(prm['sgu_w'][l], prm['sgu_b'][l], DEC_Q)
    y_c, c_v = _sgu(p, prm['sgu_norm'][l], mix, bias_tile)

    y_b = _paged_attn(pt, lp['lam'], _masked_queries(per_req(cols('b_q')), 2 * N_HEADS, DIFF_DQ, DIFF_DQ ** -0.5),
                      _decode_add_tiles(prm['rel_bias'][:, :N_HEADS], past_len, 2), _pad_new(per_req(b_k)),
                      _pad_new(per_req(b_v)), jnp.tile(prm['diff_norm'][l], N_HEADS), None,
                      past['k_diff'], past['v_diff'], l, diff=True, out_scale=1.0 - lp['lam_init'])

    qi = per_req(cols('d_qi')).reshape(nreq, DEC_Q, IDX_HEADS, IDX_DIM)
    wqi = jnp.swapaxes(qi, 1, 2).reshape(nreq, IDX_HEADS * DEC_Q, IDX_DIM).astype(BF16)
    w = per_req(tail[:, IDX_DIM:IDX_DIM + IDX_HEADS] * (IDX_HEADS ** -0.5 * IDX_DIM ** -0.5))
    wcol = jnp.broadcast_to(jnp.swapaxes(w, 1, 2).reshape(nreq, IDX_HEADS * DEC_Q, 1), (nreq, IDX_HEADS * DEC_Q, PAGE))
    sel = _dsa_decode_select(pt, wqi, wcol, _pad_new(per_req(d_ki)), past['k_index'], l)
    y_d = _paged_attn(pt, lp['lam'], _masked_queries(per_req(cols('d_q')), N_HEADS, HEAD_DV, HEAD_DV ** -0.5),
                      _decode_add_tiles(prm['rel_bias'][:, N_HEADS:], past_len, 1), _pad_new(per_req(d_k)),
                      _pad_new(per_req(d_v)), jnp.ones((GROUP_W,), F32), sel,
                      past['k_sparse'], past['v_sparse'], l, diff=False, out_scale=1.0)

    x_new = _post(x2d, (y_a, y_b, y_c, y_d), prm['mix_scale'][l], lp['w_out'], prm['ffn_norm'][l], lp['w_gu'],
                  lp['w_down'], prm['final_norm'], final)
    hd = lambda a: a.reshape(nreq, DEC_Q, N_HEADS, HEAD_DV)
    return x_new, (hd(b_k), hd(b_v), hd(d_k), hd(d_v), d_ki.reshape(nreq, DEC_Q, IDX_DIM), _blockdiag_to_state(st),
                   c_v.reshape(nreq, DEC_Q, GROUP_W))


def kernel(x_prompt, x_sample, cache_k_diff, cache_v_diff, cache_k_sparse, cache_v_sparse, cache_k_index, state_hgrn, page_table, w_in, w_out, attn_norm, ffn_norm, final_norm, w_gate_up, w_down, hgrn_lb_logits, hgrn_norm, diff_lambda_q1, diff_lambda_k1, diff_lambda_q2, diff_lambda_k2, diff_norm, sgu_w, sgu_b, sgu_norm, mix_scale, rel_bias):
    prm = dict(w_in=w_in, w_out=w_out, attn_norm=attn_norm, ffn_norm=ffn_norm, final_norm=final_norm,
               w_gate_up=w_gate_up, w_down=w_down, hgrn_lb_logits=hgrn_lb_logits, hgrn_norm=hgrn_norm,
               lq1=diff_lambda_q1, lk1=diff_lambda_k1, lq2=diff_lambda_q2, lk2=diff_lambda_k2, diff_norm=diff_norm,
               sgu_w=sgu_w, sgu_b=sgu_b, sgu_norm=sgu_norm, mix_scale=mix_scale, rel_bias=rel_bias)
    past = dict(page_table=page_table, k_diff=cache_k_diff, v_diff=cache_v_diff, k_sparse=cache_k_sparse,
                v_sparse=cache_v_sparse, k_index=cache_k_index, hgrn=state_hgrn)
    depth = w_in.shape[0]
    bsz, seq, d = x_prompt.shape
    nreq = x_sample.shape[0]
    xp = x_prompt.reshape(bsz * seq, d)
    xs = x_sample.reshape(nreq * DEC_Q, d)
    st_p, st_s = [], []
    for l in range(depth):
        lp = _layer_params(l, prm)
        final = l == depth - 1
        xp, sp = _layer_prompt(xp, bsz, seq, l, prm, lp, final)
        xs, ss = _layer_decode(xs, nreq, l, prm, lp, past, final)
        st_p.append(sp)
        st_s.append(ss)
    stk = lambda lst, i: jnp.stack([s[i] for s in lst], axis=0)
    return (xp.reshape(bsz, seq, d), xs.reshape(nreq, DEC_Q, d),
            stk(st_p, 0), stk(st_p, 1), stk(st_p, 2), stk(st_p, 3), stk(st_p, 4), stk(st_p, 5),
            stk(st_s, 0), stk(st_s, 1), stk(st_s, 2), stk(st_s, 3), stk(st_s, 4), stk(st_s, 5), stk(st_s, 6))
```

```python
import functools
import math

import numpy as np
import jax
import jax.numpy as jnp
from jax import lax
from jax.experimental import pallas as pl
from jax.experimental.pallas import tpu as pltpu

F32 = jnp.float32
BF16 = jnp.bfloat16
I32 = jnp.int32

D_MODEL = 1024
GROUP_W = 256
N_HEADS = 4
HEAD_DV = 64
DIFF_DQ = 32
IDX_HEADS = 8
IDX_DIM = 32
TOPK_MAX = 256
REL_BUCKETS = 32
REL_EXACT = 16
REL_MAX_DIST = 128
PAGE = 128
D_FF = 2816
EPS = 1e-6
D_IN = 3368
D_IN_PAD = 3456
COL = {'a_q': 0, 'a_f': 1, 'a_i': 2, 'a_g': 3, 'b_q': 4, 'b_k': 5, 'b_v': 6, 'c_u': 7, 'c_v': 8,
       'd_q': 9, 'd_k': 10, 'd_v': 11, 'd_qi': 12}
TAIL_COL128 = 26
NEG = -1e30
INT_MIN = -2 ** 31
TQ = 128
HG = 16
VMEM_LIMIT = 56 * 1024 * 1024


def _cparams(sem):
    return pltpu.CompilerParams(dimension_semantics=sem, vmem_limit_bytes=VMEM_LIMIT)


def _dot(a, b):
    return jnp.dot(a, b, preferred_element_type=F32)


def _dot_nt(a, b):
    return lax.dot_general(a, b, (((1,), (1,)), ((), ())), preferred_element_type=F32)


def _dot_tn(a, b):
    return lax.dot_general(a, b, (((0,), (0,)), ((), ())), preferred_element_type=F32)


def _sigmoid(x):
    return 1.0 / (1.0 + jnp.exp(-x))


def _split3(x):
    hi = x.astype(BF16)
    r = x - hi.astype(F32)
    mid = r.astype(BF16)
    lo = (r - mid.astype(F32)).astype(BF16)
    return hi, mid, lo


def _dot_exact_lhs(m_bf16, x):
    hi, mid, lo = _split3(x)
    return _dot(m_bf16, hi) + _dot(m_bf16, mid) + _dot(m_bf16, lo)


def _dot_exact_rhs(x, m_bf16):
    hi, mid, lo = _split3(x)
    return _dot(hi, m_bf16) + _dot(mid, m_bf16) + _dot(lo, m_bf16)


def _proj_body(x_ref, g_ref, w_ref, o_ref):
    x = x_ref[...]
    h = x * lax.rsqrt(jnp.mean(x * x, axis=-1, keepdims=True) + EPS) * g_ref[...]
    o_ref[...] = _dot(h.astype(BF16), w_ref[...])


def _proj(x2d, gain, w_bf16):
    n, d = x2d.shape
    dn = w_bf16.shape[1]
    tm = min(256, n)
    return pl.pallas_call(
        _proj_body,
        grid=(n // tm,),
        in_specs=[pl.BlockSpec((tm, d), lambda i: (i, 0)),
                  pl.BlockSpec((1, d), lambda i: (0, 0)),
                  pl.BlockSpec((d, dn), lambda i: (0, 0))],
        out_specs=pl.BlockSpec((tm, dn), lambda i: (i, 0)),
        out_shape=jax.ShapeDtypeStruct((n, dn), F32),
        compiler_params=_cparams(("parallel",)),
        name="proj",
    )(x2d, gain.reshape(1, d), w_bf16)


FF_CHUNK = 256


def _outproj_body(x_ref, ya_ref, yb_ref, yc_ref, yd_ref, ms_ref, wo_ref, o_ref):
    acc = x_ref[...]
    for i, y_ref in enumerate((ya_ref, yb_ref, yc_ref, yd_ref)):
        lo, hi = GROUP_W * i, GROUP_W * (i + 1)
        yi = (y_ref[...] * ms_ref[:, lo:hi]).astype(BF16)
        acc = acc + _dot(yi, wo_ref[lo:hi, :])
    o_ref[...] = acc


def _ffn_body(x_ref, fn_ref, wgu_ref, wd_ref, fin_ref, o_ref, acc_ref, *, final):
    x = x_ref[...]
    acc_ref[...] = x
    h = (x * lax.rsqrt(jnp.mean(x * x, axis=-1, keepdims=True) + EPS) * fn_ref[...]).astype(BF16)
    for c in range(D_FF // FF_CHUNK):
        lo, hi = FF_CHUNK * c, FF_CHUNK * (c + 1)
        g = _dot(h, wgu_ref[:, lo:hi])
        u = _dot(h, wgu_ref[:, D_FF + lo:D_FF + hi])
        a = (g * _sigmoid(g) * u).astype(BF16)
        acc_ref[...] += _dot(a, wd_ref[lo:hi, :])
    out = acc_ref[...]
    if final:
        out = out * lax.rsqrt(jnp.mean(out * out, axis=-1, keepdims=True) + EPS) * fin_ref[...]
    o_ref[...] = out


def _post(x2d, ys, mix_scale, wo, ffn_norm, wgu, wd, final_norm, final):
    n, d = x2d.shape
    tm = min(256, n)
    row = lambda i: (i, 0)
    const = lambda i: (0, 0)
    x1 = pl.pallas_call(
        _outproj_body,
        grid=(n // tm,),
        in_specs=[pl.BlockSpec((tm, d), row)]
        + [pl.BlockSpec((tm, GROUP_W), row) for _ in range(4)]
        + [pl.BlockSpec((1, d), const), pl.BlockSpec((d, d), const)],
        out_specs=pl.BlockSpec((tm, d), row),
        out_shape=jax.ShapeDtypeStruct((n, d), F32),
        compiler_params=_cparams(("parallel",)),
        name="outproj",
    )(x2d, *ys, mix_scale.reshape(1, d), wo)
    return pl.pallas_call(
        functools.partial(_ffn_body, final=final),
        grid=(n // tm,),
        in_specs=[pl.BlockSpec((tm, d), row),
                  pl.BlockSpec((1, d), const),
                  pl.BlockSpec((d, 2 * D_FF), const),
                  pl.BlockSpec((D_FF, d), const),
                  pl.BlockSpec((1, d), const)],
        out_specs=pl.BlockSpec((tm, d), row),
        out_shape=jax.ShapeDtypeStruct((n, d), F32),
        scratch_shapes=[pltpu.VMEM((tm, d), F32)],
        compiler_params=_cparams(("parallel",)),
        name="ffn",
    )(x1, ffn_norm.reshape(1, d), wgu, wd, final_norm.reshape(1, d))


def _hgrn_consts(tr):
    r = np.arange(tr)
    same = (r[:, None] // HG) == (r[None, :] // HG)
    tri = same & (r[None, :] <= r[:, None])
    c = np.arange(GROUP_W)
    heads = (c[:, None] // HEAD_DV) == (c[None, :] // HEAD_DV)
    return (jnp.asarray(tri, BF16), jnp.asarray(same, BF16), jnp.asarray(heads, BF16), jnp.asarray(heads, F32))


def _shift_rows(x, d, row):
    if d == 0:
        return x
    return jnp.where(row >= d, pltpu.roll(x, d, axis=0), 0.0)


def _hgrn_body(aq_ref, af_ref, ai_ref, ag_ref, la_ref, l1_ref, hn_ref, tri_ref, same_ref, e_ref, mask_ref, st0_ref,
               ya_ref, st_ref, cum_ref, tot_ref, kk_ref, qq_ref, qe_ref, ke_ref, o_ref, *, seq, tr, valid_rows):
    st_ref[...] = st0_ref[...]

    def prep(t, carry):
        r0 = pl.multiple_of(t * tr, tr)
        rows = pl.ds(r0, tr)
        z = af_ref[rows, :]
        ls = jnp.minimum(z, 0.0) - jnp.log(1.0 + jnp.exp(-jnp.abs(z)))
        b = l1_ref[...] + ls
        a = la_ref[...]
        lf = jnp.maximum(a, b) + jnp.log(1.0 + jnp.exp(-jnp.abs(a - b)))
        kk = 1.0 - jnp.exp(lf)
        if valid_rows < seq:
            ok = (lax.broadcasted_iota(I32, (tr, GROUP_W), 0) + r0) < valid_rows
            lf = jnp.where(ok, lf, 0.0)
            kk = jnp.where(ok, kk, 0.0)
        cum = _dot_exact_lhs(tri_ref[...], lf)
        tot = _dot_exact_lhs(same_ref[...], lf)
        q = aq_ref[rows, :]
        qq = q * _sigmoid(q)
        cum_ref[rows, :] = cum
        tot_ref[rows, :] = tot
        kk_ref[rows, :] = kk
        qq_ref[rows, :] = qq
        qe_ref[rows, :] = (qq * jnp.exp(cum)).astype(BF16)
        ke_ref[rows, :] = (kk * jnp.exp(tot - cum)).astype(BF16)
        return carry

    lax.fori_loop(0, seq // tr, prep, 0)

    row = lax.broadcasted_iota(I32, (HG, GROUP_W), 0)

    def block(i, carry):
        rows = pl.ds(pl.multiple_of(i * HG, HG), HG)
        cum = cum_ref[rows, :]
        kk = kk_ref[rows, :]
        qq = qq_ref[rows, :]
        v = ai_ref[rows, :]
        o = _dot_nt(qe_ref[rows, :], st_ref[...].astype(BF16))
        for d in range(HG):
            x = qq * _shift_rows(kk, d, row) * jnp.exp(cum - _shift_rows(cum, d, row))
            o = o + _dot(x.astype(BF16), e_ref[...]) * _shift_rows(v, d, row)
        o_ref[rows, :] = o
        dec = jnp.exp(tot_ref[pl.ds(pl.multiple_of(i * HG, HG), 1), :])
        upd = _dot_tn(v.astype(BF16), ke_ref[rows, :])
        st_ref[...] = st_ref[...] * dec + mask_ref[...] * upd
        return carry

    lax.fori_loop(0, seq // HG, block, 0, unroll=min(2, seq // HG))

    def fin(t, carry):
        rows = pl.ds(pl.multiple_of(t * tr, tr), tr)
        o = o_ref[rows, :]
        hi = (o * o).astype(BF16)
        lo = (o * o - hi.astype(F32)).astype(BF16)
        msq = (_dot(hi, e_ref[...]) + _dot(lo, e_ref[...])) * (1.0 / HEAD_DV)
        g = ag_ref[rows, :]
        ya_ref[rows, :] = o * lax.rsqrt(msq + EPS) * hn_ref[...] * (g * _sigmoid(g))
        return carry

    lax.fori_loop(0, seq // tr, fin, 0)


def _hgrn(p2d, bsz, seq, log_lb, log1m_lb, hnorm, st0, valid_rows):
    tr = min(128, seq)
    tri, same, e_bf, mask = _hgrn_consts(tr)
    col = lambda name: pl.BlockSpec((seq, GROUP_W), lambda b, c=COL[name]: (b, c))
    vec = pl.BlockSpec((1, GROUP_W), lambda b: (0, 0))
    mat = lambda n: pl.BlockSpec((n, n), lambda b: (0, 0))
    st_spec = pl.BlockSpec((None, GROUP_W, GROUP_W), lambda b: (b, 0, 0))
    big = lambda dt: pltpu.VMEM((seq, GROUP_W), dt)
    return pl.pallas_call(
        functools.partial(_hgrn_body, seq=seq, tr=tr, valid_rows=valid_rows),
        grid=(bsz,),
        in_specs=[col('a_q'), col('a_f'), col('a_i'), col('a_g'), vec, vec, vec, mat(tr), mat(tr),
                  mat(GROUP_W), mat(GROUP_W), st_spec],
        out_specs=[pl.BlockSpec((seq, GROUP_W), lambda b: (b, 0)), st_spec],
        out_shape=[jax.ShapeDtypeStruct((bsz * seq, GROUP_W), F32),
                   jax.ShapeDtypeStruct((bsz, GROUP_W, GROUP_W), F32)],
        scratch_shapes=[big(F32), big(F32), big(F32), big(F32), big(BF16), big(BF16), big(F32)],
        compiler_params=_cparams(("parallel",)),
        name="hgrn",
    )(p2d, p2d, p2d, p2d, log_lb.reshape(1, GROUP_W), log1m_lb.reshape(1, GROUP_W), hnorm.reshape(1, GROUP_W),
      tri, same, e_bf, mask, st0)


def _state_to_blockdiag(s):
    st = jnp.swapaxes(s, 2, 3)
    eye = jnp.eye(N_HEADS, dtype=s.dtype)
    return jnp.einsum('bhvk,hg->bhvgk', st, eye).reshape(s.shape[0], GROUP_W, GROUP_W)


def _blockdiag_to_state(st):
    b = st.shape[0]
    st5 = st.reshape(b, N_HEADS, HEAD_DV, N_HEADS, HEAD_DV)
    diag = jnp.stack([st5[:, h, :, h, :] for h in range(N_HEADS)], axis=1)
    return jnp.swapaxes(diag, 2, 3)


def _gelu(x):
    return 0.5 * x * (1.0 + lax.erf(x * (2.0 ** -0.5)))


def _sgu_body(cu_ref, cv_ref, sn_ref, mix_ref, bias_ref, yc_ref, cvn_ref, *, tr):
    u = _gelu(cu_ref[...])
    v = _gelu(cv_ref[...])
    vn = v * lax.rsqrt(jnp.mean(v * v, axis=-1, keepdims=True) + EPS) * sn_ref[...]
    cvn_ref[...] = vn
    vb = vn.astype(BF16)
    lane = lax.broadcasted_iota(I32, (tr, GROUP_W), 1)
    mixed = bias_ref[...]
    for g in range(N_HEADS):
        mg = _dot(mix_ref[g], vb)
        mixed = mixed + jnp.where((lane >= HEAD_DV * g) & (lane < HEAD_DV * (g + 1)), mg, 0.0)
    yc_ref[...] = u * mixed


def _sgu(p2d, snorm, mix_bf16, bias_tile):
    n = p2d.shape[0]
    tr = 128
    col = lambda name: pl.BlockSpec((tr, GROUP_W), lambda i, c=COL[name]: (i, c))
    out = pl.BlockSpec((tr, GROUP_W), lambda i: (i, 0))
    return pl.pallas_call(
        functools.partial(_sgu_body, tr=tr),
        grid=(n // tr,),
        in_specs=[col('c_u'), col('c_v'),
                  pl.BlockSpec((1, GROUP_W), lambda i: (0, 0)),
                  pl.BlockSpec((N_HEADS, tr, tr), lambda i: (0, 0, 0)),
                  pl.BlockSpec((tr, GROUP_W), lambda i: (0, 0))],
        out_specs=[out, out],
        out_shape=[jax.ShapeDtypeStruct((n, GROUP_W), F32)] * 2,
        compiler_params=_cparams(("parallel",)),
        name="sgu",
    )(p2d, p2d, snorm.reshape(1, GROUP_W), mix_bf16, bias_tile)


def _sgu_params(w, b, seq):
    c = min(128, seq)
    wt = jnp.tril(w[:, :c, :c])
    mix = jnp.einsum('rs,gtu->grtsu', jnp.eye(128 // c, dtype=w.dtype), wt).reshape(N_HEADS, 128, 128)
    bt = jnp.tile(jnp.transpose(b[:, :c]), (128 // c, 1))
    return mix.astype(BF16), jnp.repeat(bt, HEAD_DV, axis=1)


def _rel_bucket(rel):
    n = jnp.maximum(rel, 0)
    nf = jnp.maximum(n, 1).astype(F32)
    large = REL_EXACT + (jnp.log(nf / REL_EXACT) / math.log(REL_MAX_DIST / REL_EXACT)
                         * (REL_BUCKETS - REL_EXACT)).astype(I32)
    large = jnp.minimum(large, REL_BUCKETS - 1)
    return jnp.where(n < REL_EXACT, n, large)


def _bucket_lookup(tab, rel):
    onehot = (_rel_bucket(rel)[..., None] == jnp.arange(REL_BUCKETS, dtype=I32)).astype(F32)
    return jnp.einsum('...b,bh->...h', onehot, tab.astype(F32), precision=lax.Precision.HIGHEST)


def _bias_tiles(tab):
    assert REL_MAX_DIST <= TQ
    q = np.arange(TQ)[:, None]
    k = np.arange(TQ)[None, :]
    tiles = []
    for back in range(3):
        rel = jnp.asarray(q - k + TQ * back, I32)
        t = jnp.transpose(_bucket_lookup(tab, rel), (2, 0, 1))
        if back == 0:
            t = jnp.where(jnp.asarray(k <= q)[None], t, NEG)
        tiles.append(t)
    return jnp.stack(tiles, axis=0)


def _lane_band(x, lo, width):
    lane = lax.broadcasted_iota(I32, x.shape, x.ndim - 1)
    return jnp.where((lane >= lo) & (lane < lo + width), x, 0.0)


TK = 2 * TQ


def _value_blocks_t(v2d, bsz, seq, tk):
    v = jnp.transpose(v2d.reshape(bsz, seq // tk, tk, N_HEADS, HEAD_DV), (0, 1, 3, 4, 2))
    return jnp.concatenate([v, jnp.ones_like(v)], axis=3).astype(BF16)


def _bias_tiles_t(bias, maps):
    t = jnp.transpose(bias, (0, 3, 1, 2))[:, :, :, None, :]
    t = jnp.broadcast_to(t, (3, TQ, N_HEADS, maps, TQ)).reshape(3, TQ, N_HEADS * maps * TQ)
    return jnp.concatenate([t, jnp.full_like(t[:1], NEG)], axis=0)


def _step_bias(bias_ref, back):
    first = bias_ref[jnp.minimum(back, 2)]
    second = bias_ref[jnp.where(back < 1, 3, jnp.minimum(back - 1, 2))]
    return jnp.concatenate([first, second], axis=0)


def _diff_body(lam_ref, q_ref, k_ref, vt_ref, bias_ref, gain_ref, o_ref, qm_ref, m_ref, acc_ref, *, out_scale):
    i = pl.program_id(1)
    q = q_ref[...] * (DIFF_DQ ** -0.5)
    for hc in range(2 * N_HEADS):
        qm_ref[hc * TQ:(hc + 1) * TQ, :] = _lane_band(q, DIFF_DQ * hc, DIFF_DQ).astype(BF16)
    m_ref[...] = jnp.full(m_ref.shape, NEG, F32)
    acc_ref[...] = jnp.zeros(acc_ref.shape, F32)

    def step(t, carry):
        s = (_dot_nt(k_ref[pl.ds(pl.multiple_of(t * TK, TK), TK), :], qm_ref[...])
             + _step_bias(bias_ref, i - 2 * t))
        m_old = m_ref[...]
        m_new = jnp.maximum(m_old, jnp.max(s, axis=0, keepdims=True))
        alpha = jnp.exp(m_old - m_new)
        p = jnp.exp(s - m_new).astype(BF16)
        for h in range(N_HEADS):
            cols = slice(2 * h * TQ, (2 * h + 2) * TQ)
            acc_ref[h] = alpha[:, cols] * acc_ref[h] + _dot(vt_ref[t, h], p[:, cols])
        m_ref[...] = m_new
        return carry

    lax.fori_loop(0, (i + 2) // 2, step, 0)

    lam = lam_ref[0]
    res = []
    for h in range(N_HEADS):
        a = acc_ref[h]
        n = a[:HEAD_DV, :] / a[HEAD_DV:HEAD_DV + 1, :]
        o = n[:, :TQ] - lam * n[:, TQ:]
        msq = jnp.mean(o * o, axis=0, keepdims=True)
        res.append(o * lax.rsqrt(msq + EPS) * gain_ref[...] * out_scale)
    o_ref[...] = jnp.transpose(jnp.concatenate(res, axis=0))


def _diff_prompt(p2d, bsz, seq, k_bf16, v_blocks_t, lam, bias, gain, out_scale):
    nq = seq // TQ
    assert nq % 2 == 0
    ncol = 2 * N_HEADS * TQ
    return pl.pallas_call(
        functools.partial(_diff_body, out_scale=out_scale),
        grid=(bsz, nq),
        in_specs=[pl.BlockSpec(memory_space=pltpu.SMEM),
                  pl.BlockSpec((TQ, GROUP_W), lambda b, i: (b * nq + i, COL['b_q'])),
                  pl.BlockSpec((None, seq, GROUP_W), lambda b, i: (b, 0, 0)),
                  pl.BlockSpec((None, seq // TK, N_HEADS, 2 * HEAD_DV, TK), lambda b, i: (b, 0, 0, 0, 0)),
                  pl.BlockSpec((4, TQ, ncol), lambda b, i: (0, 0, 0)),
                  pl.BlockSpec((HEAD_DV, TQ), lambda b, i: (0, 0))],
        out_specs=pl.BlockSpec((TQ, GROUP_W), lambda b, i: (b * nq + i, 0)),
        out_shape=jax.ShapeDtypeStruct((bsz * seq, GROUP_W), F32),
        scratch_shapes=[pltpu.VMEM((ncol, GROUP_W), BF16),
                        pltpu.VMEM((1, ncol), F32),
                        pltpu.VMEM((N_HEADS, 2 * HEAD_DV, 2 * TQ), F32)],
        compiler_params=_cparams(("parallel", "parallel")),
        name="diff_prompt",
    )(lam.reshape(1), p2d, k_bf16, v_blocks_t, _bias_tiles_t(bias, 2),
      jnp.broadcast_to(gain[:, None], (HEAD_DV, TQ)))


def _sort_key(score):
    bits = lax.bitcast_convert_type(score, I32)
    return jnp.where(bits < 0, bits ^ 0x7FFFFFFF, bits)


def _kth_largest(count_ge, n_sel, shape):
    t = jnp.full(shape, INT_MIN, I32)
    for bit in range(31, -1, -1):
        step = INT_MIN if bit == 31 else (1 << bit)
        cand = t + jnp.int32(step)
        t = jnp.where(count_ge(cand) >= n_sel, cand, t)
    return t


def _dsa_body(q_ref, qi_ref, wt_ref, k_ref, vt_ref, ki_ref, bias_ref, tri_ref, o_ref,
              qm_ref, qim_ref, key_ref, sel_ref, m_ref, acc_ref, *, n_sel):
    i = pl.program_id(1)
    q = q_ref[...] * (HEAD_DV ** -0.5)
    for h in range(N_HEADS):
        qm_ref[h * TQ:(h + 1) * TQ, :] = _lane_band(q, HEAD_DV * h, HEAD_DV).astype(BF16)
    qi = qi_ref[...]
    for e in range(IDX_HEADS):
        qim_ref[e * TQ:(e + 1) * TQ, :] = _lane_band(qi, IDX_DIM * e, IDX_DIM).astype(BF16)
    w = wt_ref[...] * (IDX_HEADS ** -0.5 * IDX_DIM ** -0.5)
    krow = lax.broadcasted_iota(I32, (TK, TQ), 0)
    qcol = lax.broadcasted_iota(I32, (TK, TQ), 1)
    n_step = (i + 2) // 2

    def score(t, carry):
        s = jnp.maximum(_dot_nt(ki_ref[pl.ds(pl.multiple_of(t * TK, TK), TK), :], qim_ref[...]), 0.0)
        sc = s[:, 0:TQ] * w[0:1, :]
        for e in range(1, IDX_HEADS):
            sc = sc + s[:, e * TQ:(e + 1) * TQ] * w[e:e + 1, :]
        causal = (krow + t * TK) <= (qcol + i * TQ)
        key_ref[t] = jnp.where(causal, _sort_key(sc), INT_MIN)
        return carry

    lax.fori_loop(0, n_step, score, 0)

    def count(pred):
        def body(t, c):
            hit = jnp.where(pred(key_ref[t]), 1.0, 0.0)
            return c + jnp.sum(hit.reshape(TK // 8, 8, TQ), axis=0)
        return jnp.sum(lax.fori_loop(0, n_step, body, jnp.zeros((8, TQ), F32)), axis=0, keepdims=True)

    thr = _kth_largest(lambda cand: count(lambda k: k >= cand), float(n_sel), (1, TQ))
    no_surplus = jnp.max(count(lambda k: k >= thr)) <= float(n_sel)

    @pl.when(no_surplus)
    def _():
        def select(t, carry):
            sel_ref[t] = jnp.where(key_ref[t] >= thr, 0.0, NEG)
            return carry

        lax.fori_loop(0, n_step, select, 0)

    @pl.when(jnp.logical_not(no_surplus))
    def _():
        room = float(n_sel) - count(lambda k: k > thr)

        def select(t, seen):
            k = key_ref[t]
            eq = jnp.where(k == thr, 1.0, 0.0)
            rank = seen + _dot(tri_ref[...], eq.astype(BF16))
            keep = ((k > thr) | ((k == thr) & (rank <= room))) & (k != INT_MIN)
            sel_ref[t] = jnp.where(keep, 0.0, NEG)
            return seen + jnp.sum(eq, axis=0, keepdims=True)

        lax.fori_loop(0, n_step, select, jnp.zeros((1, TQ), F32))

    m_ref[...] = jnp.full(m_ref.shape, NEG, F32)
    acc_ref[...] = jnp.zeros(acc_ref.shape, F32)

    def attend(t, carry):
        sel = sel_ref[t]
        s = (_dot_nt(k_ref[pl.ds(pl.multiple_of(t * TK, TK), TK), :], qm_ref[...]) + _step_bias(bias_ref, i - 2 * t)
             + jnp.concatenate([sel] * N_HEADS, axis=1))
        m_old = m_ref[...]
        m_new = jnp.maximum(m_old, jnp.max(s, axis=0, keepdims=True))
        alpha = jnp.exp(m_old - m_new)
        p = jnp.exp(s - m_new).astype(BF16)
        for h in range(N_HEADS):
            cols = slice(h * TQ, (h + 1) * TQ)
            acc_ref[h] = alpha[:, cols] * acc_ref[h] + _dot(vt_ref[t, h], p[:, cols])
        m_ref[...] = m_new
        return carry

    lax.fori_loop(0, n_step, attend, 0)
    out_t = jnp.concatenate([acc_ref[h][:HEAD_DV, :] / acc_ref[h][HEAD_DV:HEAD_DV + 1, :] for h in range(N_HEADS)],
                            axis=0)
    o_ref[...] = jnp.transpose(out_t)


def _dsa_prompt(p2d, bsz, seq, k_bf16, v_blocks_t, ki_rep, w_t, bias):
    nq = seq // TQ
    assert nq % 2 == 0
    n_sel = min(TOPK_MAX, seq // 4)
    tri = jnp.asarray(np.arange(TK)[None, :] <= np.arange(TK)[:, None], BF16)
    qcol = lambda c: pl.BlockSpec((TQ, GROUP_W), lambda b, i, c=c: (b * nq + i, c))
    full = pl.BlockSpec((None, seq, GROUP_W), lambda b, i: (b, 0, 0))
    return pl.pallas_call(
        functools.partial(_dsa_body, n_sel=n_sel),
        grid=(bsz, nq),
        in_specs=[qcol(COL['d_q']), qcol(COL['d_qi']),
                  pl.BlockSpec((IDX_HEADS, TQ), lambda b, i: (0, b * nq + i)),
                  full,
                  pl.BlockSpec((None, seq // TK, N_HEADS, 2 * HEAD_DV, TK), lambda b, i: (b, 0, 0, 0, 0)),
                  full,
                  pl.BlockSpec((4, TQ, N_HEADS * TQ), lambda b, i: (0, 0, 0)),
                  pl.BlockSpec((TK, TK), lambda b, i: (0, 0))],
        out_specs=pl.BlockSpec((TQ, GROUP_W), lambda b, i: (b * nq + i, 0)),
        out_shape=jax.ShapeDtypeStruct((bsz * seq, GROUP_W), F32),
        scratch_shapes=[pltpu.VMEM((N_HEADS * TQ, GROUP_W), BF16),
                        pltpu.VMEM((IDX_HEADS * TQ, GROUP_W), BF16),
                        pltpu.VMEM((seq // TK, TK, TQ), I32),
                        pltpu.VMEM((seq // TK, TK, TQ), F32),
                        pltpu.VMEM((1, N_HEADS * TQ), F32),
                        pltpu.VMEM((N_HEADS, 2 * HEAD_DV, TQ), F32)],
        compiler_params=_cparams(("parallel", "parallel")),
        name="dsa_prompt",
    )(p2d, p2d, w_t, k_bf16, v_blocks_t, ki_rep, _bias_tiles_t(bias, 1), tri)


DEC_Q = 8
PPS_ATTN = 32
PPS_SELECT = 32
PAGE_GROUP = 8


def _paged_update(s, vts, m_ref, l_ref, acc_ref):
    n = len(vts)
    m_old = m_ref[...]
    m_new = jnp.maximum(m_old, jnp.max(s, axis=-1, keepdims=True))
    alpha = jnp.exp(m_old - m_new)
    p = jnp.exp(s - jnp.concatenate([m_new] * n, axis=1))
    l = alpha * l_ref[...]
    acc = jnp.concatenate([alpha, alpha], axis=1) * acc_ref[...]
    for i in range(n):
        pi = p[:, i * PAGE:(i + 1) * PAGE]
        l = l + pi
        acc = acc + _dot_nt(pi.astype(BF16), vts[i])
    l_ref[...] = l
    acc_ref[...] = acc
    m_ref[...] = m_new


def _paged_body(pt_ref, lam_ref, wq_ref, add_ref, knew_ref, vnew_ref, gain_ref, e_ref, *rest,
                n_pages, pps, rows, diff, use_sel, out_scale):
    del pt_ref
    if use_sel:
        sel_ref, selnew_ref = rest[:2]
        rest = rest[2:]
    k_refs, v_refs = rest[:pps], rest[pps:2 * pps]
    o_ref, m_ref, l_ref, acc_ref = rest[2 * pps:]
    g = pl.program_id(1)

    @pl.when(g == 0)
    def _():
        m_ref[...] = jnp.full(m_ref.shape, NEG, F32)
        l_ref[...] = jnp.zeros(l_ref.shape, F32)
        acc_ref[...] = jnp.zeros(acc_ref.shape, F32)

    wq = wq_ref[...]
    last_step = g == n_pages // pps - 1
    far = add_ref[0]
    for lo in range(0, pps, PAGE_GROUP):
        parts = []
        for i in range(lo, min(lo + PAGE_GROUP, pps)):
            si = _dot(wq, k_refs[i][...].astype(BF16))
            si = si + (far if i < pps - 1 else jnp.where(last_step, add_ref[1], far))
            if use_sel:
                si = si + jnp.tile(sel_ref[i], (rows // DEC_Q, 1))
            parts.append(si)
        _paged_update(jnp.concatenate(parts, axis=1),
                      [v_refs[i][...].astype(BF16) for i in range(lo, min(lo + PAGE_GROUP, pps))],
                      m_ref, l_ref, acc_ref)

    @pl.when(last_step)
    def _():
        s = _dot(wq, knew_ref[...]) + add_ref[2]
        if use_sel:
            s = s + jnp.tile(selnew_ref[0], (rows // DEC_Q, 1))
        _paged_update(s, [vnew_ref[...]], m_ref, l_ref, acc_ref)
        o_all = acc_ref[...] / jnp.sum(l_ref[...], axis=-1, keepdims=True)
        out = jnp.zeros((DEC_Q, GROUP_W), F32)
        for h in range(N_HEADS):
            if diff:
                lo = 2 * h * DEC_Q
                o_h = o_all[lo:lo + DEC_Q] - lam_ref[0] * o_all[lo + DEC_Q:lo + 2 * DEC_Q]
            else:
                o_h = o_all[h * DEC_Q:(h + 1) * DEC_Q]
            out = out + _lane_band(o_h, HEAD_DV * h, HEAD_DV)
        if diff:
            sq = out * out
            hi = sq.astype(BF16)
            lo_ = (sq - hi.astype(F32)).astype(BF16)
            msq = (_dot(hi, e_ref[...]) + _dot(lo_, e_ref[...])) * (1.0 / HEAD_DV)
            out = out * lax.rsqrt(msq + EPS) * gain_ref[...] * out_scale
        o_ref[...] = out


def _paged_attn(page_table, lam, wq, add, k_new, v_new, gain256, sel, cache_k, cache_v, l, *, diff, out_scale):
    nreq, n_pages = page_table.shape
    rows = wq.shape[1]
    n_pool = cache_k.shape[1]
    ck = _feature_major_pages(cache_k)
    cv = _feature_major_pages(cache_v)
    use_sel = sel is not None
    pps = min(PPS_ATTN, n_pages)
    page_spec = lambda i: pl.BlockSpec((None, None, GROUP_W, PAGE),
                                       lambda r, g, pt, i=i: (l, pt[r * n_pages + g * pps + i], 0, 0))
    per_req = lambda shape: pl.BlockSpec((None,) + shape, lambda r, g, pt: (r,) + (0,) * len(shape))
    const = lambda shape: pl.BlockSpec(shape, lambda r, g, pt: (0,) * len(shape))
    in_specs = [pl.BlockSpec(memory_space=pltpu.SMEM), per_req((rows, GROUP_W)), const((3, rows, PAGE)),
                per_req((GROUP_W, PAGE)), per_req((GROUP_W, PAGE)), const((1, GROUP_W)), const((GROUP_W, GROUP_W))]
    args = [lam.reshape(1), wq, add, k_new, v_new, gain256.reshape(1, GROUP_W), _hgrn_consts(128)[2]]
    if use_sel:
        in_specs += [pl.BlockSpec((None, pps, DEC_Q, PAGE), lambda r, g, pt: (r, g, 0, 0)),
                     pl.BlockSpec((None, 1, DEC_Q, PAGE), lambda r, g, pt: (r, n_pages, 0, 0))]
        args += [sel, sel]
    in_specs += [page_spec(i) for i in range(pps)] * 2
    args += [ck] * pps + [cv] * pps
    return pl.pallas_call(
        functools.partial(_paged_body, n_pages=n_pages, pps=pps, rows=rows, diff=diff, use_sel=use_sel,
                          out_scale=out_scale),
        grid_spec=pltpu.PrefetchScalarGridSpec(
            num_scalar_prefetch=1,
            grid=(nreq, n_pages // pps),
            in_specs=in_specs,
            out_specs=pl.BlockSpec((DEC_Q, GROUP_W), lambda r, g, pt: (r, 0)),
            scratch_shapes=[pltpu.VMEM((rows, PAGE), F32), pltpu.VMEM((rows, PAGE), F32),
                            pltpu.VMEM((rows, GROUP_W), F32)]),
        out_shape=jax.ShapeDtypeStruct((nreq * DEC_Q, GROUP_W), F32),
        compiler_params=_cparams(("parallel", "arbitrary")),
        name="paged_diff" if diff else "paged_dsa",
    )(page_table.reshape(-1), *args)


def _decode_add_tiles(tab, past, groups_per_head):
    t = np.arange(DEC_Q)[:, None]
    c = np.arange(PAGE)[None, :]
    far = jnp.asarray(np.broadcast_to(past + t - (past - PAGE - 1), (DEC_Q, PAGE)), I32)
    last = jnp.asarray(t + PAGE - c, I32)
    new = jnp.asarray(t - c, I32)
    ok_new = jnp.asarray((c <= t) & (c < DEC_Q))
    tiles = []
    for rel, ok in ((far, None), (last, None), (new, ok_new)):
        b = jnp.transpose(_bucket_lookup(tab, rel), (2, 0, 1))
        if ok is not None:
            b = jnp.where(ok[None], b, NEG)
        tiles.append(jnp.repeat(b, groups_per_head, axis=0).reshape(-1, PAGE))
    return jnp.stack(tiles, axis=0)


def _masked_queries(q, n_groups, width, scale):
    lane = np.arange(GROUP_W)[None, :] // width == np.arange(n_groups)[:, None]
    m = jnp.asarray(lane, F32)[None, :, None, :] * (q * scale)[:, None, :, :]
    return m.reshape(q.shape[0], n_groups * DEC_Q, GROUP_W).astype(BF16)


def _pad_new(a):
    return jnp.pad(jnp.swapaxes(a, 1, 2), ((0, 0), (0, 0), (0, PAGE - DEC_Q))).astype(BF16)


def _feature_major_pages(cache):
    d, n_pool = cache.shape[:2]
    return jnp.swapaxes(cache.reshape(d, n_pool, PAGE, -1), 2, 3)


SEL_CHUNK = 32


def _dsa_sel_body(pt_ref, wqi_ref, wcol_ref, kinew_ref, tri_ref, *rest, n_pages, pps, n_pad, n_sel):
    del pt_ref
    ki_refs = rest[:pps]
    sel_ref, key_ref = rest[pps:]
    g = pl.program_id(1)
    wqi = wqi_ref[...]
    wcol = wcol_ref[...]

    def score(kts):
        kt = kts[0] if len(kts) == 1 else jnp.concatenate(kts, axis=1)
        s = jnp.maximum(_dot(wqi, kt), 0.0) * jnp.concatenate([wcol] * len(kts), axis=1)
        out = s[0:DEC_Q]
        for e in range(1, IDX_HEADS):
            out = out + s[e * DEC_Q:(e + 1) * DEC_Q]
        return out

    keys = _sort_key(score([ki_refs[i][...].astype(BF16) for i in range(pps)]))
    for i in range(pps):
        rows = pl.ds(pl.multiple_of((g * pps + i) * DEC_Q, DEC_Q), DEC_Q)
        key_ref[rows, :] = keys[:, i * PAGE:(i + 1) * PAGE]

    @pl.when(g == n_pages // pps - 1)
    def _():
        t = lax.broadcasted_iota(I32, (DEC_Q, PAGE), 0)
        c = lax.broadcasted_iota(I32, (DEC_Q, PAGE), 1)
        key_ref[n_pages * DEC_Q:(n_pages + 1) * DEC_Q, :] = jnp.where(c <= t, _sort_key(score([kinew_ref[...]])), INT_MIN)
        if n_pad > n_pages + 1:
            key_ref[(n_pages + 1) * DEC_Q:, :] = jnp.full(((n_pad - n_pages - 1) * DEC_Q, PAGE), INT_MIN, I32)

        def count(pred):
            tot = jnp.zeros((DEC_Q, PAGE), F32)
            for ch in range(n_pad // SEL_CHUNK):
                hit = jnp.where(pred(key_ref[ch * SEL_CHUNK * DEC_Q:(ch + 1) * SEL_CHUNK * DEC_Q, :]), 1.0, 0.0)
                for k in range(SEL_CHUNK):
                    tot = tot + hit[k * DEC_Q:(k + 1) * DEC_Q]
            return jnp.sum(tot, axis=-1, keepdims=True)

        tile_rows = lambda x: jnp.tile(jnp.broadcast_to(x, (DEC_Q, PAGE)), (SEL_CHUNK, 1))
        thr = _kth_largest(lambda cand: count(lambda k, cb=tile_rows(cand): k >= cb), float(n_sel), (DEC_Q, 1))
        n_ge = count(lambda k, tb=tile_rows(thr): k >= tb)
        no_surplus = jnp.max(n_ge) <= float(n_sel)

        @pl.when(no_surplus)
        def _():
            thr_b = jnp.broadcast_to(thr, (DEC_Q, PAGE))
            for j in range(n_pages + 1):
                sel_ref[j] = jnp.where(key_ref[j * DEC_Q:(j + 1) * DEC_Q, :] >= thr_b, 0.0, NEG)

        @pl.when(jnp.logical_not(no_surplus))
        def _():
            room = float(n_sel) - count(lambda k, tb=tile_rows(thr): k > tb)

            def select(j, seen):
                k = key_ref[pl.ds(pl.multiple_of(j * DEC_Q, DEC_Q), DEC_Q), :]
                eq = jnp.where(k == thr, 1.0, 0.0)
                rank = seen + _dot(eq.astype(BF16), tri_ref[...])
                keep = (k > thr) | ((k == thr) & (rank <= room))
                sel_ref[j] = jnp.where(keep, 0.0, NEG)
                return seen + jnp.sum(eq, axis=-1, keepdims=True)

            lax.fori_loop(0, n_pages + 1, select, jnp.zeros((DEC_Q, 1), F32))


def _dsa_decode_select(page_table, wqi, wcol, ki_new, cache_ki, l):
    nreq, n_pages = page_table.shape
    n_sel = min(TOPK_MAX, (n_pages * PAGE + DEC_Q) // 4)
    n_pad = -(-(n_pages + 1) // SEL_CHUNK) * SEL_CHUNK
    tri = jnp.asarray(np.arange(PAGE)[:, None] <= np.arange(PAGE)[None, :], BF16)
    rows = IDX_HEADS * DEC_Q
    pps = min(PPS_SELECT, n_pages)
    page_spec = lambda i: pl.BlockSpec((None, None, IDX_DIM, PAGE),
                                       lambda r, g, pt, i=i: (l, pt[r * n_pages + g * pps + i], 0, 0))
    cache_ki = _feature_major_pages(cache_ki)
    return pl.pallas_call(
        functools.partial(_dsa_sel_body, n_pages=n_pages, pps=pps, n_pad=n_pad, n_sel=n_sel),
        grid_spec=pltpu.PrefetchScalarGridSpec(
            num_scalar_prefetch=1,
            grid=(nreq, n_pages // pps),
            in_specs=[pl.BlockSpec((None, rows, IDX_DIM), lambda r, g, pt: (r, 0, 0)),
                      pl.BlockSpec((None, rows, PAGE), lambda r, g, pt: (r, 0, 0)),
                      pl.BlockSpec((None, IDX_DIM, PAGE), lambda r, g, pt: (r, 0, 0)),
                      pl.BlockSpec((PAGE, PAGE), lambda r, g, pt: (0, 0))]
            + [page_spec(i) for i in range(pps)],
            out_specs=pl.BlockSpec((None, n_pages + 1, DEC_Q, PAGE), lambda r, g, pt: (r, 0, 0, 0)),
            scratch_shapes=[pltpu.VMEM((n_pad * DEC_Q, PAGE), I32)]),
        out_shape=jax.ShapeDtypeStruct((nreq, n_pages + 1, DEC_Q, PAGE), F32),
        compiler_params=_cparams(("parallel", "arbitrary")),
        name="dsa_select",
    )(page_table.reshape(-1), wqi, wcol, ki_new, tri, *([cache_ki] * pps))


def _layer_params(l, prm):
    f32 = F32
    lb_all = jnp.cumsum(jax.nn.softmax(prm['hgrn_lb_logits'].astype(f32), axis=0), axis=0)
    lb = lb_all[l] - lb_all[0]
    lam_init = 0.8 - 0.6 * math.exp(-0.3 * l)
    lam = (jnp.exp(jnp.sum(prm['lq1'][l] * prm['lk1'][l])) - jnp.exp(jnp.sum(prm['lq2'][l] * prm['lk2'][l])) + lam_init)
    w_in = jnp.pad(prm['w_in'][l], ((0, 0), (0, D_IN_PAD - D_IN))).astype(BF16)
    return dict(
        w_in=w_in, log_lb=jnp.maximum(jnp.log(lb), NEG), log1m_lb=jnp.log1p(-lb), lam=lam.astype(f32), lam_init=lam_init,
        bias_b=_bias_tiles(prm['rel_bias'][:, :N_HEADS]), bias_d=_bias_tiles(prm['rel_bias'][:, N_HEADS:]),
        w_out=prm['w_out'][l].astype(BF16), w_gu=prm['w_gate_up'][l].astype(BF16), w_down=prm['w_down'][l].astype(BF16))


def _layer_prompt(x2d, bsz, seq, l, prm, lp, final):
    n = bsz * seq
    p = _proj(x2d, prm['attn_norm'][l], lp['w_in'])
    cols = lambda name: p[:, GROUP_W * COL[name]:GROUP_W * (COL[name] + 1)]
    b_k, b_v, d_k, d_v = cols('b_k'), cols('b_v'), cols('d_k'), cols('d_v')
    d_ki = p[:, 128 * TAIL_COL128:128 * TAIL_COL128 + IDX_DIM]

    st0 = jnp.zeros((bsz, GROUP_W, GROUP_W), F32)
    y_a, st = _hgrn(p, bsz, seq, lp['log_lb'], lp['log1m_lb'], prm['hgrn_norm'][l], st0, seq)

    y_b = _diff_prompt(p, bsz, seq, b_k.astype(BF16).reshape(bsz, seq, GROUP_W), _value_blocks_t(b_v, bsz, seq, TK),
                       lp['lam'], lp['bias_b'], prm['diff_norm'][l], 1.0 - lp['lam_init'])

    mix, bias_tile = _sgu_params(prm['sgu_w'][l], prm['sgu_b'][l], seq)
    y_c, c_v = _sgu(p, prm['sgu_norm'][l], mix, bias_tile)

    ki_rep = jnp.tile(d_ki, (1, IDX_HEADS)).astype(BF16).reshape(bsz, seq, GROUP_W)
    w_t = jnp.transpose(p[:, 128 * TAIL_COL128 + IDX_DIM:128 * TAIL_COL128 + IDX_DIM + IDX_HEADS])
    y_d = _dsa_prompt(p, bsz, seq, d_k.astype(BF16).reshape(bsz, seq, GROUP_W), _value_blocks_t(d_v, bsz, seq, TK),
                      ki_rep, w_t, lp['bias_d'])

    x_new = _post(x2d, (y_a, y_b, y_c, y_d), prm['mix_scale'][l], lp['w_out'], prm['ffn_norm'][l], lp['w_gu'],
                  lp['w_down'], prm['final_norm'], final)
    hd = lambda a: a.reshape(bsz, seq, N_HEADS, HEAD_DV)
    return x_new, (hd(b_k), hd(b_v), hd(d_k), hd(d_v), d_ki.reshape(bsz, seq, IDX_DIM), _blockdiag_to_state(st),
                   c_v.reshape(bsz, seq, GROUP_W))


def _layer_decode(x2d, nreq, l, prm, lp, past, final):
    n = nreq * DEC_Q
    pt = past['page_table']
    past_len = pt.shape[1] * PAGE
    p = _proj(x2d, prm['attn_norm'][l], lp['w_in'])
    cols = lambda name: p[:, GROUP_W * COL[name]:GROUP_W * (COL[name] + 1)]
    per_req = lambda a: a.reshape(nreq, DEC_Q, a.shape[-1])
    b_k, b_v, d_k, d_v = cols('b_k'), cols('b_v'), cols('d_k'), cols('d_v')
    tail = p[:, 128 * TAIL_COL128:128 * (TAIL_COL128 + 1)]
    d_ki = tail[:, :IDX_DIM]

    p16 = jnp.pad(per_req(p), ((0, 0), (0, HG - DEC_Q), (0, 0))).reshape(nreq * HG, D_IN_PAD)
    y_a16, st = _hgrn(p16, nreq, HG, lp['log_lb'], lp['log1m_lb'], prm['hgrn_norm'][l],
                      _state_to_blockdiag(past['hgrn'][l]), DEC_Q)
    y_a = y_a16.reshape(nreq, HG, GROUP_W)[:, :DEC_Q].reshape(n, GROUP_W)

    mix, bias_tile = _sgu_params(prm['sgu_w'][l], prm['sgu_b'][l], DEC_Q)
    y_c, c_v = _sgu(p, prm['sgu_norm'][l], mix, bias_tile)

    y_b = _paged_attn(pt, lp['lam'], _masked_queries(per_req(cols('b_q')), 2 * N_HEADS, DIFF_DQ, DIFF_DQ ** -0.5),
                      _decode_add_tiles(prm['rel_bias'][:, :N_HEADS], past_len, 2), _pad_new(per_req(b_k)),
                      _pad_new(per_req(b_v)), jnp.tile(prm['diff_norm'][l], N_HEADS), None,
                      past['k_diff'], past['v_diff'], l, diff=True, out_scale=1.0 - lp['lam_init'])

    qi = per_req(cols('d_qi')).reshape(nreq, DEC_Q, IDX_HEADS, IDX_DIM)
    wqi = jnp.swapaxes(qi, 1, 2).reshape(nreq, IDX_HEADS * DEC_Q, IDX_DIM).astype(BF16)
    w = per_req(tail[:, IDX_DIM:IDX_DIM + IDX_HEADS] * (IDX_HEADS ** -0.5 * IDX_DIM ** -0.5))
    wcol = jnp.broadcast_to(jnp.swapaxes(w, 1, 2).reshape(nreq, IDX_HEADS * DEC_Q, 1), (nreq, IDX_HEADS * DEC_Q, PAGE))
    sel = _dsa_decode_select(pt, wqi, wcol, _pad_new(per_req(d_ki)), past['k_index'], l)
    y_d = _paged_attn(pt, lp['lam'], _masked_queries(per_req(cols('d_q')), N_HEADS, HEAD_DV, HEAD_DV ** -0.5),
                      _decode_add_tiles(prm['rel_bias'][:, N_HEADS:], past_len, 1), _pad_new(per_req(d_k)),
                      _pad_new(per_req(d_v)), jnp.ones((GROUP_W,), F32), sel,
                      past['k_sparse'], past['v_sparse'], l, diff=False, out_scale=1.0)

    x_new = _post(x2d, (y_a, y_b, y_c, y_d), prm['mix_scale'][l], lp['w_out'], prm['ffn_norm'][l], lp['w_gu'],
                  lp['w_down'], prm['final_norm'], final)
    hd = lambda a: a.reshape(nreq, DEC_Q, N_HEADS, HEAD_DV)
    return x_new, (hd(b_k), hd(b_v), hd(d_k), hd(d_v), d_ki.reshape(nreq, DEC_Q, IDX_DIM), _blockdiag_to_state(st),
                   c_v.reshape(nreq, DEC_Q, GROUP_W))


def kernel(x_prompt, x_sample, cache_k_diff, cache_v_diff, cache_k_sparse, cache_v_sparse, cache_k_index, state_hgrn, page_table, w_in, w_out, attn_norm, ffn_norm, final_norm, w_gate_up, w_down, hgrn_lb_logits, hgrn_norm, diff_lambda_q1, diff_lambda_k1, diff_lambda_q2, diff_lambda_k2, diff_norm, sgu_w, sgu_b, sgu_norm, mix_scale, rel_bias):
    prm = dict(w_in=w_in, w_out=w_out, attn_norm=attn_norm, ffn_norm=ffn_norm, final_norm=final_norm,
               w_gate_up=w_gate_up, w_down=w_down, hgrn_lb_logits=hgrn_lb_logits, hgrn_norm=hgrn_norm,
               lq1=diff_lambda_q1, lk1=diff_lambda_k1, lq2=diff_lambda_q2, lk2=diff_lambda_k2, diff_norm=diff_norm,
               sgu_w=sgu_w, sgu_b=sgu_b, sgu_norm=sgu_norm, mix_scale=mix_scale, rel_bias=rel_bias)
    past = dict(page_table=page_table, k_diff=cache_k_diff, v_diff=cache_v_diff, k_sparse=cache_k_sparse,
                v_sparse=cache_v_sparse, k_index=cache_k_index, hgrn=state_hgrn)
    depth = w_in.shape[0]
    bsz, seq, d = x_prompt.shape
    nreq = x_sample.shape[0]
    xp = x_prompt.reshape(bsz * seq, d)
    xs = x_sample.reshape(nreq * DEC_Q, d)
    st_p, st_s = [], []
    for l in range(depth):
        lp = _layer_params(l, prm)
        final = l == depth - 1
        xp, sp = _layer_prompt(xp, bsz, seq, l, prm, lp, final)
        xs, ss = _layer_decode(xs, nreq, l, prm, lp, past, final)
        st_p.append(sp)
        st_s.append(ss)
    stk = lambda lst, i: jnp.stack([s[i] for s in lst], axis=0)
    return (xp.reshape(bsz, seq, d), xs.reshape(nreq, DEC_Q, d),
            stk(st_p, 0), stk(st_p, 1), stk(st_p, 2), stk(st_p, 3), stk(st_p, 4), stk(st_p, 5),
            stk(st_s, 0), stk(st_s, 1), stk(st_s, 2), stk(st_s, 3), stk(st_s, 4), stk(st_s, 5), stk(st_s, 6))
```

```python
import functools
import math

import numpy as np
import jax
import jax.numpy as jnp
from jax import lax
from jax.experimental import pallas as pl
from jax.experimental.pallas import tpu as pltpu

F32 = jnp.float32
BF16 = jnp.bfloat16
I32 = jnp.int32

D_MODEL = 1024
GROUP_W = 256
N_HEADS = 4
HEAD_DV = 64
DIFF_DQ = 32
IDX_HEADS = 8
IDX_DIM = 32
TOPK_MAX = 256
REL_BUCKETS = 32
REL_EXACT = 16
REL_MAX_DIST = 128
PAGE = 128
D_FF = 2816
EPS = 1e-6
D_IN = 3368
D_IN_PAD = 3456
COL = {'a_q': 0, 'a_f': 1, 'a_i': 2, 'a_g': 3, 'b_q': 4, 'b_k': 5, 'b_v': 6, 'c_u': 7, 'c_v': 8,
       'd_q': 9, 'd_k': 10, 'd_v': 11, 'd_qi': 12}
TAIL_COL128 = 26
NEG = -1e30
INT_MIN = -2 ** 31
TQ = 128
HG = 16
HGRN_PAIR = 2
HGRN_ROWS = 512
VMEM_LIMIT = 56 * 1024 * 1024


def _cparams(sem):
    return pltpu.CompilerParams(dimension_semantics=sem, vmem_limit_bytes=VMEM_LIMIT)


def _dot(a, b):
    return jnp.dot(a, b, preferred_element_type=F32)


def _dot_nt(a, b):
    return lax.dot_general(a, b, (((1,), (1,)), ((), ())), preferred_element_type=F32)


def _dot_tn(a, b):
    return lax.dot_general(a, b, (((0,), (0,)), ((), ())), preferred_element_type=F32)


def _sigmoid(x):
    return 1.0 / (1.0 + jnp.exp(-x))


def _split3(x):
    hi = x.astype(BF16)
    r = x - hi.astype(F32)
    mid = r.astype(BF16)
    lo = (r - mid.astype(F32)).astype(BF16)
    return hi, mid, lo


def _dot_exact_lhs(m_bf16, x):
    hi, mid, lo = _split3(x)
    return _dot(m_bf16, hi) + _dot(m_bf16, mid) + _dot(m_bf16, lo)


def _dot_exact_rhs(x, m_bf16):
    hi, mid, lo = _split3(x)
    return _dot(hi, m_bf16) + _dot(mid, m_bf16) + _dot(lo, m_bf16)


def _proj_body(x_ref, g_ref, w_ref, o_ref):
    x = x_ref[...]
    h = x * lax.rsqrt(jnp.mean(x * x, axis=-1, keepdims=True) + EPS) * g_ref[...]
    o_ref[...] = _dot(h.astype(BF16), w_ref[...])


def _proj(x2d, gain, w_bf16):
    n, d = x2d.shape
    dn = w_bf16.shape[1]
    tm = min(256, n)
    return pl.pallas_call(
        _proj_body,
        grid=(n // tm,),
        in_specs=[pl.BlockSpec((tm, d), lambda i: (i, 0)),
                  pl.BlockSpec((1, d), lambda i: (0, 0)),
                  pl.BlockSpec((d, dn), lambda i: (0, 0))],
        out_specs=pl.BlockSpec((tm, dn), lambda i: (i, 0)),
        out_shape=jax.ShapeDtypeStruct((n, dn), F32),
        compiler_params=_cparams(("parallel",)),
        name="proj",
    )(x2d, gain.reshape(1, d), w_bf16)


FF_CHUNK = 256


def _outproj_body(x_ref, ya_ref, yb_ref, yc_ref, yd_ref, ms_ref, wo_ref, o_ref):
    acc = x_ref[...]
    for i, y_ref in enumerate((ya_ref, yb_ref, yc_ref, yd_ref)):
        lo, hi = GROUP_W * i, GROUP_W * (i + 1)
        yi = (y_ref[...] * ms_ref[:, lo:hi]).astype(BF16)
        acc = acc + _dot(yi, wo_ref[lo:hi, :])
    o_ref[...] = acc


def _ffn_body(x_ref, fn_ref, wgu_ref, wd_ref, fin_ref, o_ref, acc_ref, *, final):
    x = x_ref[...]
    acc_ref[...] = x
    h = (x * lax.rsqrt(jnp.mean(x * x, axis=-1, keepdims=True) + EPS) * fn_ref[...]).astype(BF16)
    for c in range(D_FF // FF_CHUNK):
        lo, hi = FF_CHUNK * c, FF_CHUNK * (c + 1)
        g = _dot(h, wgu_ref[:, lo:hi])
        u = _dot(h, wgu_ref[:, D_FF + lo:D_FF + hi])
        a = (g * _sigmoid(g) * u).astype(BF16)
        acc_ref[...] += _dot(a, wd_ref[lo:hi, :])
    out = acc_ref[...]
    if final:
        out = out * lax.rsqrt(jnp.mean(out * out, axis=-1, keepdims=True) + EPS) * fin_ref[...]
    o_ref[...] = out


def _post(x2d, ys, mix_scale, wo, ffn_norm, wgu, wd, final_norm, final):
    n, d = x2d.shape
    tm = min(256, n)
    row = lambda i: (i, 0)
    const = lambda i: (0, 0)
    x1 = pl.pallas_call(
        _outproj_body,
        grid=(n // tm,),
        in_specs=[pl.BlockSpec((tm, d), row)]
        + [pl.BlockSpec((tm, GROUP_W), row) for _ in range(4)]
        + [pl.BlockSpec((1, d), const), pl.BlockSpec((d, d), const)],
        out_specs=pl.BlockSpec((tm, d), row),
        out_shape=jax.ShapeDtypeStruct((n, d), F32),
        compiler_params=_cparams(("parallel",)),
        name="outproj",
    )(x2d, *ys, mix_scale.reshape(1, d), wo)
    return pl.pallas_call(
        functools.partial(_ffn_body, final=final),
        grid=(n // tm,),
        in_specs=[pl.BlockSpec((tm, d), row),
                  pl.BlockSpec((1, d), const),
                  pl.BlockSpec((d, 2 * D_FF), const),
                  pl.BlockSpec((D_FF, d), const),
                  pl.BlockSpec((1, d), const)],
        out_specs=pl.BlockSpec((tm, d), row),
        out_shape=jax.ShapeDtypeStruct((n, d), F32),
        scratch_shapes=[pltpu.VMEM((tm, d), F32)],
        compiler_params=_cparams(("parallel",)),
        name="ffn",
    )(x1, ffn_norm.reshape(1, d), wgu, wd, final_norm.reshape(1, d))


def _hgrn_consts(tr):
    r = np.arange(tr)
    same = (r[:, None] // HG) == (r[None, :] // HG)
    tri = same & (r[None, :] <= r[:, None])
    c = np.arange(GROUP_W)
    heads = (c[:, None] // HEAD_DV) == (c[None, :] // HEAD_DV)
    return (jnp.asarray(tri, BF16), jnp.asarray(same, BF16), jnp.asarray(heads, BF16), jnp.asarray(heads, F32))


def _shift_rows(x, d, row):
    if d == 0:
        return x
    return jnp.where(row >= d, pltpu.roll(x, d, axis=0), 0.0)


def _hgrn_body(aq_ref, af_ref, ai_ref, ag_ref, la_ref, l1_ref, hn_ref, tri_ref, same_ref, e_ref, mask_ref, st0_ref,
               ya_ref, st_ref, cum_ref, tot_ref, kk_ref, qq_ref, qe_ref, ke_ref, o_ref, *, sc, tr, valid_rows, seq):
    c = pl.program_id(1)

    @pl.when(c == 0)
    def _():
        st_ref[...] = st0_ref[...]

    def prep(t, carry):
        r0 = pl.multiple_of(t * tr, tr)
        rows = pl.ds(r0, tr)
        for e in range(HGRN_PAIR):
            z = af_ref[e, rows, :]
            ls = jnp.minimum(z, 0.0) - jnp.log(1.0 + jnp.exp(-jnp.abs(z)))
            b = l1_ref[...] + ls
            a = la_ref[...]
            lf = jnp.maximum(a, b) + jnp.log(1.0 + jnp.exp(-jnp.abs(a - b)))
            kk = 1.0 - jnp.exp(lf)
            if valid_rows < seq:
                ok = (lax.broadcasted_iota(I32, (tr, GROUP_W), 0) + r0 + c * sc) < valid_rows
                lf = jnp.where(ok, lf, 0.0)
                kk = jnp.where(ok, kk, 0.0)
            cum = _dot_exact_lhs(tri_ref[...], lf)
            tot = _dot_exact_lhs(same_ref[...], lf)
            q = aq_ref[e, rows, :]
            qq = q * _sigmoid(q)
            cum_ref[e, rows, :] = cum
            tot_ref[e, rows, :] = tot
            kk_ref[e, rows, :] = kk
            qq_ref[e, rows, :] = qq
            qe_ref[e, rows, :] = (qq * jnp.exp(cum)).astype(BF16)
            ke_ref[e, rows, :] = (kk * jnp.exp(tot - cum)).astype(BF16)
        return carry

    lax.fori_loop(0, sc // tr, prep, 0)

    row = lax.broadcasted_iota(I32, (HG, GROUP_W), 0)

    def block(i, carry):
        rows = pl.ds(pl.multiple_of(i * HG, HG), HG)
        for e in range(HGRN_PAIR):
            cum = cum_ref[e, rows, :]
            kk = kk_ref[e, rows, :]
            qq = qq_ref[e, rows, :]
            v = ai_ref[e, rows, :]
            o = _dot_nt(qe_ref[e, rows, :], st_ref[e].astype(BF16))
            for d in range(HG):
                x = qq * _shift_rows(kk, d, row) * jnp.exp(cum - _shift_rows(cum, d, row))
                o = o + _dot(x.astype(BF16), e_ref[...]) * _shift_rows(v, d, row)
            o_ref[e, rows, :] = o
            dec = jnp.exp(tot_ref[e, pl.ds(pl.multiple_of(i * HG, HG), 1), :])
            upd = _dot_tn(v.astype(BF16), ke_ref[e, rows, :])
            st_ref[e] = st_ref[e] * dec + mask_ref[...] * upd
        return carry

    lax.fori_loop(0, sc // HG, block, 0)

    def fin(t, carry):
        rows = pl.ds(pl.multiple_of(t * tr, tr), tr)
        for e in range(HGRN_PAIR):
            o = o_ref[e, rows, :]
            hi = (o * o).astype(BF16)
            lo = (o * o - hi.astype(F32)).astype(BF16)
            msq = (_dot(hi, e_ref[...]) + _dot(lo, e_ref[...])) * (1.0 / HEAD_DV)
            g = ag_ref[e, rows, :]
            ya_ref[e, rows, :] = o * lax.rsqrt(msq + EPS) * hn_ref[...] * (g * _sigmoid(g))
        return carry

    lax.fori_loop(0, sc // tr, fin, 0)


def _hgrn(p2d, bsz, seq, log_lb, log1m_lb, hnorm, st0, valid_rows):
    assert bsz % HGRN_PAIR == 0
    sc = min(HGRN_ROWS, seq)
    tr = min(128, sc)
    tri, same, e_bf, mask = _hgrn_consts(tr)
    col = lambda name: pl.BlockSpec((HGRN_PAIR, sc, GROUP_W), lambda b, c, k=COL[name]: (b, c, k))
    vec = pl.BlockSpec((1, GROUP_W), lambda b, c: (0, 0))
    mat = lambda n: pl.BlockSpec((n, n), lambda b, c: (0, 0))
    st_spec = pl.BlockSpec((HGRN_PAIR, GROUP_W, GROUP_W), lambda b, c: (b, 0, 0))
    big = lambda dt: pltpu.VMEM((HGRN_PAIR, sc, GROUP_W), dt)
    p3d = p2d.reshape(bsz, seq, p2d.shape[1])
    ya, st = pl.pallas_call(
        functools.partial(_hgrn_body, sc=sc, tr=tr, valid_rows=valid_rows, seq=seq),
        grid=(bsz // HGRN_PAIR, seq // sc),
        in_specs=[col('a_q'), col('a_f'), col('a_i'), col('a_g'), vec, vec, vec, mat(tr), mat(tr),
                  mat(GROUP_W), mat(GROUP_W), st_spec],
        out_specs=[pl.BlockSpec((HGRN_PAIR, sc, GROUP_W), lambda b, c: (b, c, 0)), st_spec],
        out_shape=[jax.ShapeDtypeStruct((bsz, seq, GROUP_W), F32),
                   jax.ShapeDtypeStruct((bsz, GROUP_W, GROUP_W), F32)],
        scratch_shapes=[big(F32), big(F32), big(F32), big(F32), big(BF16), big(BF16), big(F32)],
        compiler_params=_cparams(("parallel", "arbitrary")),
        name="hgrn",
    )(p3d, p3d, p3d, p3d, log_lb.reshape(1, GROUP_W), log1m_lb.reshape(1, GROUP_W), hnorm.reshape(1, GROUP_W),
      tri, same, e_bf, mask, st0)
    return ya.reshape(bsz * seq, GROUP_W), st


def _state_to_blockdiag(s):
    st = jnp.swapaxes(s, 2, 3)
    eye = jnp.eye(N_HEADS, dtype=s.dtype)
    return jnp.einsum('bhvk,hg->bhvgk', st, eye).reshape(s.shape[0], GROUP_W, GROUP_W)


def _blockdiag_to_state(st):
    b = st.shape[0]
    st5 = st.reshape(b, N_HEADS, HEAD_DV, N_HEADS, HEAD_DV)
    diag = jnp.stack([st5[:, h, :, h, :] for h in range(N_HEADS)], axis=1)
    return jnp.swapaxes(diag, 2, 3)


def _gelu(x):
    return 0.5 * x * (1.0 + lax.erf(x * (2.0 ** -0.5)))


def _sgu_body(cu_ref, cv_ref, sn_ref, mix_ref, bias_ref, yc_ref, cvn_ref, *, tr):
    u = _gelu(cu_ref[...])
    v = _gelu(cv_ref[...])
    vn = v * lax.rsqrt(jnp.mean(v * v, axis=-1, keepdims=True) + EPS) * sn_ref[...]
    cvn_ref[...] = vn
    vb = vn.astype(BF16)
    lane = lax.broadcasted_iota(I32, (tr, GROUP_W), 1)
    mixed = bias_ref[...]
    for g in range(N_HEADS):
        mg = _dot(mix_ref[g], vb)
        mixed = mixed + jnp.where((lane >= HEAD_DV * g) & (lane < HEAD_DV * (g + 1)), mg, 0.0)
    yc_ref[...] = u * mixed


def _sgu(p2d, snorm, mix_bf16, bias_tile):
    n = p2d.shape[0]
    tr = 128
    col = lambda name: pl.BlockSpec((tr, GROUP_W), lambda i, c=COL[name]: (i, c))
    out = pl.BlockSpec((tr, GROUP_W), lambda i: (i, 0))
    return pl.pallas_call(
        functools.partial(_sgu_body, tr=tr),
        grid=(n // tr,),
        in_specs=[col('c_u'), col('c_v'),
                  pl.BlockSpec((1, GROUP_W), lambda i: (0, 0)),
                  pl.BlockSpec((N_HEADS, tr, tr), lambda i: (0, 0, 0)),
                  pl.BlockSpec((tr, GROUP_W), lambda i: (0, 0))],
        out_specs=[out, out],
        out_shape=[jax.ShapeDtypeStruct((n, GROUP_W), F32)] * 2,
        compiler_params=_cparams(("parallel",)),
        name="sgu",
    )(p2d, p2d, snorm.reshape(1, GROUP_W), mix_bf16, bias_tile)


def _sgu_params(w, b, seq):
    c = min(128, seq)
    wt = jnp.tril(w[:, :c, :c])
    mix = jnp.einsum('rs,gtu->grtsu', jnp.eye(128 // c, dtype=w.dtype), wt).reshape(N_HEADS, 128, 128)
    bt = jnp.tile(jnp.transpose(b[:, :c]), (128 // c, 1))
    return mix.astype(BF16), jnp.repeat(bt, HEAD_DV, axis=1)


def _rel_bucket(rel):
    n = jnp.maximum(rel, 0)
    nf = jnp.maximum(n, 1).astype(F32)
    large = REL_EXACT + (jnp.log(nf / REL_EXACT) / math.log(REL_MAX_DIST / REL_EXACT)
                         * (REL_BUCKETS - REL_EXACT)).astype(I32)
    large = jnp.minimum(large, REL_BUCKETS - 1)
    return jnp.where(n < REL_EXACT, n, large)


def _bucket_lookup(tab, rel):
    onehot = (_rel_bucket(rel)[..., None] == jnp.arange(REL_BUCKETS, dtype=I32)).astype(F32)
    return jnp.einsum('...b,bh->...h', onehot, tab.astype(F32), precision=lax.Precision.HIGHEST)


def _bias_tiles(tab):
    assert REL_MAX_DIST <= TQ
    q = np.arange(TQ)[:, None]
    k = np.arange(TQ)[None, :]
    tiles = []
    for back in range(3):
        rel = jnp.asarray(q - k + TQ * back, I32)
        t = jnp.transpose(_bucket_lookup(tab, rel), (2, 0, 1))
        if back == 0:
            t = jnp.where(jnp.asarray(k <= q)[None], t, NEG)
        tiles.append(t)
    return jnp.stack(tiles, axis=0)


def _lane_band(x, lo, width):
    lane = lax.broadcasted_iota(I32, x.shape, x.ndim - 1)
    return jnp.where((lane >= lo) & (lane < lo + width), x, 0.0)


TK = 2 * TQ
COUNT_ROWS = 32


def _value_blocks_t(v2d, bsz, seq, tk):
    v = jnp.transpose(v2d.reshape(bsz, seq // tk, tk, N_HEADS, HEAD_DV), (0, 1, 3, 4, 2))
    return jnp.concatenate([v, jnp.ones_like(v)], axis=3).astype(BF16)


def _bias_tiles_t(bias, maps):
    t = jnp.transpose(bias, (0, 3, 1, 2))[:, :, :, None, :]
    t = jnp.broadcast_to(t, (3, TQ, N_HEADS, maps, TQ)).reshape(3, TQ, N_HEADS * maps * TQ)
    return jnp.concatenate([t, jnp.full_like(t[:1], NEG)], axis=0)


def _step_bias(bias_ref, back):
    first = bias_ref[jnp.minimum(back, 2)]
    second = bias_ref[jnp.where(back < 1, 3, jnp.minimum(back - 1, 2))]
    return jnp.concatenate([first, second], axis=0)


def _diff_body(lam_ref, q_ref, k_ref, vt_ref, bias_ref, gain_ref, o_ref, qm_ref, m_ref, acc_ref, *, out_scale):
    i = pl.program_id(1)
    q = q_ref[...] * (DIFF_DQ ** -0.5)
    for hc in range(2 * N_HEADS):
        qm_ref[hc * TQ:(hc + 1) * TQ, :] = _lane_band(q, DIFF_DQ * hc, DIFF_DQ).astype(BF16)
    m_ref[...] = jnp.full(m_ref.shape, NEG, F32)
    acc_ref[...] = jnp.zeros(acc_ref.shape, F32)

    def step(t, carry):
        s = (_dot_nt(k_ref[pl.ds(pl.multiple_of(t * TK, TK), TK), :], qm_ref[...])
             + _step_bias(bias_ref, i - 2 * t))
        m_old = m_ref[...]
        m_new = jnp.maximum(m_old, jnp.max(s, axis=0, keepdims=True))
        alpha = jnp.exp(m_old - m_new)
        p = jnp.exp(s - m_new).astype(BF16)
        for h in range(N_HEADS):
            cols = slice(2 * h * TQ, (2 * h + 2) * TQ)
            acc_ref[h] = alpha[:, cols] * acc_ref[h] + _dot(vt_ref[t, h], p[:, cols])
        m_ref[...] = m_new
        return carry

    lax.fori_loop(0, (i + 2) // 2, step, 0)

    lam = lam_ref[0]
    res = []
    for h in range(N_HEADS):
        a = acc_ref[h]
        n = a[:HEAD_DV, :] / a[HEAD_DV:HEAD_DV + 1, :]
        o = n[:, :TQ] - lam * n[:, TQ:]
        msq = jnp.mean(o * o, axis=0, keepdims=True)
        res.append(o * lax.rsqrt(msq + EPS) * gain_ref[...] * out_scale)
    o_ref[...] = jnp.transpose(jnp.concatenate(res, axis=0))


def _diff_prompt(p2d, bsz, seq, k_bf16, v_blocks_t, lam, bias, gain, out_scale):
    nq = seq // TQ
    assert nq % 2 == 0
    ncol = 2 * N_HEADS * TQ
    return pl.pallas_call(
        functools.partial(_diff_body, out_scale=out_scale),
        grid=(bsz, nq),
        in_specs=[pl.BlockSpec(memory_space=pltpu.SMEM),
                  pl.BlockSpec((TQ, GROUP_W), lambda b, i: (b * nq + i, COL['b_q'])),
                  pl.BlockSpec((None, seq, GROUP_W), lambda b, i: (b, 0, 0)),
                  pl.BlockSpec((None, seq // TK, N_HEADS, 2 * HEAD_DV, TK), lambda b, i: (b, 0, 0, 0, 0)),
                  pl.BlockSpec((4, TQ, ncol), lambda b, i: (0, 0, 0)),
                  pl.BlockSpec((HEAD_DV, TQ), lambda b, i: (0, 0))],
        out_specs=pl.BlockSpec((TQ, GROUP_W), lambda b, i: (b * nq + i, 0)),
        out_shape=jax.ShapeDtypeStruct((bsz * seq, GROUP_W), F32),
        scratch_shapes=[pltpu.VMEM((ncol, GROUP_W), BF16),
                        pltpu.VMEM((1, ncol), F32),
                        pltpu.VMEM((N_HEADS, 2 * HEAD_DV, 2 * TQ), F32)],
        compiler_params=_cparams(("parallel", "parallel")),
        name="diff_prompt",
    )(lam.reshape(1), p2d, k_bf16, v_blocks_t, _bias_tiles_t(bias, 2),
      jnp.broadcast_to(gain[:, None], (HEAD_DV, TQ)))


def _sort_key(score):
    bits = lax.bitcast_convert_type(score, I32)
    return jnp.where(bits < 0, bits ^ 0x7FFFFFFF, bits)


def _kth_largest(count_ge, n_sel, shape):
    t = jnp.full(shape, INT_MIN, I32)
    for bit in range(31, -1, -1):
        step = INT_MIN if bit == 31 else (1 << bit)
        cand = t + jnp.int32(step)
        t = jnp.where(count_ge(cand) >= n_sel, cand, t)
    return t


def _dsa_body(q_ref, qi_ref, wt_ref, k_ref, vt_ref, ki_ref, bias_ref, tri_ref, o_ref,
              qm_ref, qim_ref, key_ref, sel_ref, m_ref, acc_ref, *, n_sel):
    i = pl.program_id(1)
    q = q_ref[...] * (HEAD_DV ** -0.5)
    for h in range(N_HEADS):
        qm_ref[h * TQ:(h + 1) * TQ, :] = _lane_band(q, HEAD_DV * h, HEAD_DV).astype(BF16)
    qi = qi_ref[...]
    for e in range(IDX_HEADS):
        qim_ref[e * TQ:(e + 1) * TQ, :] = _lane_band(qi, IDX_DIM * e, IDX_DIM).astype(BF16)
    w = wt_ref[...] * (IDX_HEADS ** -0.5 * IDX_DIM ** -0.5)
    krow = lax.broadcasted_iota(I32, (TK, TQ), 0)
    qcol = lax.broadcasted_iota(I32, (TK, TQ), 1)
    n_step = (i + 2) // 2

    def score(t, carry):
        s = jnp.maximum(_dot_nt(ki_ref[pl.ds(pl.multiple_of(t * TK, TK), TK), :], qim_ref[...]), 0.0)
        sc = s[:, 0:TQ] * w[0:1, :]
        for e in range(1, IDX_HEADS):
            sc = sc + s[:, e * TQ:(e + 1) * TQ] * w[e:e + 1, :]
        causal = (krow + t * TK) <= (qcol + i * TQ)
        key_ref[t] = jnp.where(causal, _sort_key(sc), INT_MIN)
        return carry

    lax.fori_loop(0, n_step, score, 0)

    def count(pred):
        def body(t, c):
            hit = jnp.where(pred(key_ref[t]), 1.0, 0.0)
            return c + jnp.sum(hit.reshape(TK // COUNT_ROWS, COUNT_ROWS, TQ), axis=0)
        return jnp.sum(lax.fori_loop(0, n_step, body, jnp.zeros((COUNT_ROWS, TQ), F32)), axis=0, keepdims=True)

    thr = _kth_largest(lambda cand: count(lambda k: k >= cand), float(n_sel), (1, TQ))
    no_surplus = jnp.max(count(lambda k: k >= thr)) <= float(n_sel)

    @pl.when(no_surplus)
    def _():
        def select(t, carry):
            sel_ref[t] = jnp.where(key_ref[t] >= thr, 0.0, NEG)
            return carry

        lax.fori_loop(0, n_step, select, 0)

    @pl.when(jnp.logical_not(no_surplus))
    def _():
        room = float(n_sel) - count(lambda k: k > thr)

        def select(t, seen):
            k = key_ref[t]
            eq = jnp.where(k == thr, 1.0, 0.0)
            rank = seen + _dot(tri_ref[...], eq.astype(BF16))
            keep = ((k > thr) | ((k == thr) & (rank <= room))) & (k != INT_MIN)
            sel_ref[t] = jnp.where(keep, 0.0, NEG)
            return seen + jnp.sum(eq, axis=0, keepdims=True)

        lax.fori_loop(0, n_step, select, jnp.zeros((1, TQ), F32))

    m_ref[...] = jnp.full(m_ref.shape, NEG, F32)
    acc_ref[...] = jnp.zeros(acc_ref.shape, F32)

    def attend(t, carry):
        sel = sel_ref[t]
        s = (_dot_nt(k_ref[pl.ds(pl.multiple_of(t * TK, TK), TK), :], qm_ref[...]) + _step_bias(bias_ref, i - 2 * t)
             + jnp.concatenate([sel] * N_HEADS, axis=1))
        m_old = m_ref[...]
        m_new = jnp.maximum(m_old, jnp.max(s, axis=0, keepdims=True))
        alpha = jnp.exp(m_old - m_new)
        p = jnp.exp(s - m_new).astype(BF16)
        for h in range(N_HEADS):
            cols = slice(h * TQ, (h + 1) * TQ)
            acc_ref[h] = alpha[:, cols] * acc_ref[h] + _dot(vt_ref[t, h], p[:, cols])
        m_ref[...] = m_new
        return carry

    lax.fori_loop(0, n_step, attend, 0)
    out_t = jnp.concatenate([acc_ref[h][:HEAD_DV, :] / acc_ref[h][HEAD_DV:HEAD_DV + 1, :] for h in range(N_HEADS)],
                            axis=0)
    o_ref[...] = jnp.transpose(out_t)


def _dsa_prompt(p2d, bsz, seq, k_bf16, v_blocks_t, ki_rep, w_t, bias):
    nq = seq // TQ
    assert nq % 2 == 0
    n_sel = min(TOPK_MAX, seq // 4)
    tri = jnp.asarray(np.arange(TK)[None, :] <= np.arange(TK)[:, None], BF16)
    qcol = lambda c: pl.BlockSpec((TQ, GROUP_W), lambda b, i, c=c: (b * nq + i, c))
    full = pl.BlockSpec((None, seq, GROUP_W), lambda b, i: (b, 0, 0))
    return pl.pallas_call(
        functools.partial(_dsa_body, n_sel=n_sel),
        grid=(bsz, nq),
        in_specs=[qcol(COL['d_q']), qcol(COL['d_qi']),
                  pl.BlockSpec((IDX_HEADS, TQ), lambda b, i: (0, b * nq + i)),
                  full,
                  pl.BlockSpec((None, seq // TK, N_HEADS, 2 * HEAD_DV, TK), lambda b, i: (b, 0, 0, 0, 0)),
                  full,
                  pl.BlockSpec((4, TQ, N_HEADS * TQ), lambda b, i: (0, 0, 0)),
                  pl.BlockSpec((TK, TK), lambda b, i: (0, 0))],
        out_specs=pl.BlockSpec((TQ, GROUP_W), lambda b, i: (b * nq + i, 0)),
        out_shape=jax.ShapeDtypeStruct((bsz * seq, GROUP_W), F32),
        scratch_shapes=[pltpu.VMEM((N_HEADS * TQ, GROUP_W), BF16),
                        pltpu.VMEM((IDX_HEADS * TQ, GROUP_W), BF16),
                        pltpu.VMEM((seq // TK, TK, TQ), I32),
                        pltpu.VMEM((seq // TK, TK, TQ), F32),
                        pltpu.VMEM((1, N_HEADS * TQ), F32),
                        pltpu.VMEM((N_HEADS, 2 * HEAD_DV, TQ), F32)],
        compiler_params=_cparams(("parallel", "parallel")),
        name="dsa_prompt",
    )(p2d, p2d, w_t, k_bf16, v_blocks_t, ki_rep, _bias_tiles_t(bias, 1), tri)


DEC_Q = 8
PPS_ATTN = 32
PPS_SELECT = 32
PAGE_GROUP = 8


def _paged_update(s, vts, m_ref, l_ref, acc_ref):
    n = len(vts)
    m_old = m_ref[...]
    m_new = jnp.maximum(m_old, jnp.max(s, axis=-1, keepdims=True))
    alpha = jnp.exp(m_old - m_new)
    p = jnp.exp(s - jnp.concatenate([m_new] * n, axis=1))
    l = alpha * l_ref[...]
    acc = jnp.concatenate([alpha, alpha], axis=1) * acc_ref[...]
    for i in range(n):
        pi = p[:, i * PAGE:(i + 1) * PAGE]
        l = l + pi
        acc = acc + _dot_nt(pi.astype(BF16), vts[i])
    l_ref[...] = l
    acc_ref[...] = acc
    m_ref[...] = m_new


def _paged_body(pt_ref, lam_ref, wq_ref, add_ref, knew_ref, vnew_ref, gain_ref, e_ref, *rest,
                n_pages, pps, rows, diff, use_sel, out_scale):
    del pt_ref
    if use_sel:
        sel_ref, selnew_ref = rest[:2]
        rest = rest[2:]
    k_refs, v_refs = rest[:pps], rest[pps:2 * pps]
    o_ref, m_ref, l_ref, acc_ref = rest[2 * pps:]
    g = pl.program_id(1)

    @pl.when(g == 0)
    def _():
        m_ref[...] = jnp.full(m_ref.shape, NEG, F32)
        l_ref[...] = jnp.zeros(l_ref.shape, F32)
        acc_ref[...] = jnp.zeros(acc_ref.shape, F32)

    wq = wq_ref[...]
    last_step = g == n_pages // pps - 1
    far = add_ref[0]
    for lo in range(0, pps, PAGE_GROUP):
        parts = []
        for i in range(lo, min(lo + PAGE_GROUP, pps)):
            si = _dot(wq, k_refs[i][...].astype(BF16))
            si = si + (far if i < pps - 1 else jnp.where(last_step, add_ref[1], far))
            if use_sel:
                si = si + jnp.tile(sel_ref[i], (rows // DEC_Q, 1))
            parts.append(si)
        _paged_update(jnp.concatenate(parts, axis=1),
                      [v_refs[i][...].astype(BF16) for i in range(lo, min(lo + PAGE_GROUP, pps))],
                      m_ref, l_ref, acc_ref)

    @pl.when(last_step)
    def _():
        s = _dot(wq, knew_ref[...]) + add_ref[2]
        if use_sel:
            s = s + jnp.tile(selnew_ref[0], (rows // DEC_Q, 1))
        _paged_update(s, [vnew_ref[...]], m_ref, l_ref, acc_ref)
        o_all = acc_ref[...] / jnp.sum(l_ref[...], axis=-1, keepdims=True)
        out = jnp.zeros((DEC_Q, GROUP_W), F32)
        for h in range(N_HEADS):
            if diff:
                lo = 2 * h * DEC_Q
                o_h = o_all[lo:lo + DEC_Q] - lam_ref[0] * o_all[lo + DEC_Q:lo + 2 * DEC_Q]
            else:
                o_h = o_all[h * DEC_Q:(h + 1) * DEC_Q]
            out = out + _lane_band(o_h, HEAD_DV * h, HEAD_DV)
        if diff:
            sq = out * out
            hi = sq.astype(BF16)
            lo_ = (sq - hi.astype(F32)).astype(BF16)
            msq = (_dot(hi, e_ref[...]) + _dot(lo_, e_ref[...])) * (1.0 / HEAD_DV)
            out = out * lax.rsqrt(msq + EPS) * gain_ref[...] * out_scale
        o_ref[...] = out


def _paged_attn(page_table, lam, wq, add, k_new, v_new, gain256, sel, cache_k, cache_v, l, *, diff, out_scale):
    nreq, n_pages = page_table.shape
    rows = wq.shape[1]
    n_pool = cache_k.shape[1]
    ck = _feature_major_pages(cache_k)
    cv = _feature_major_pages(cache_v)
    use_sel = sel is not None
    pps = min(PPS_ATTN, n_pages)
    page_spec = lambda i: pl.BlockSpec((None, None, GROUP_W, PAGE),
                                       lambda r, g, pt, i=i: (l, pt[r * n_pages + g * pps + i], 0, 0))
    per_req = lambda shape: pl.BlockSpec((None,) + shape, lambda r, g, pt: (r,) + (0,) * len(shape))
    const = lambda shape: pl.BlockSpec(shape, lambda r, g, pt: (0,) * len(shape))
    in_specs = [pl.BlockSpec(memory_space=pltpu.SMEM), per_req((rows, GROUP_W)), const((3, rows, PAGE)),
                per_req((GROUP_W, PAGE)), per_req((GROUP_W, PAGE)), const((1, GROUP_W)), const((GROUP_W, GROUP_W))]
    args = [lam.reshape(1), wq, add, k_new, v_new, gain256.reshape(1, GROUP_W), _hgrn_consts(128)[2]]
    if use_sel:
        in_specs += [pl.BlockSpec((None, pps, DEC_Q, PAGE), lambda r, g, pt: (r, g, 0, 0)),
                     pl.BlockSpec((None, 1, DEC_Q, PAGE), lambda r, g, pt: (r, n_pages, 0, 0))]
        args += [sel, sel]
    in_specs += [page_spec(i) for i in range(pps)] * 2
    args += [ck] * pps + [cv] * pps
    return pl.pallas_call(
        functools.partial(_paged_body, n_pages=n_pages, pps=pps, rows=rows, diff=diff, use_sel=use_sel,
                          out_scale=out_scale),
        grid_spec=pltpu.PrefetchScalarGridSpec(
            num_scalar_prefetch=1,
            grid=(nreq, n_pages // pps),
            in_specs=in_specs,
            out_specs=pl.BlockSpec((DEC_Q, GROUP_W), lambda r, g, pt: (r, 0)),
            scratch_shapes=[pltpu.VMEM((rows, PAGE), F32), pltpu.VMEM((rows, PAGE), F32),
                            pltpu.VMEM((rows, GROUP_W), F32)]),
        out_shape=jax.ShapeDtypeStruct((nreq * DEC_Q, GROUP_W), F32),
        compiler_params=_cparams(("parallel", "arbitrary")),
        name="paged_diff" if diff else "paged_dsa",
    )(page_table.reshape(-1), *args)


def _decode_add_tiles(tab, past, groups_per_head):
    t = np.arange(DEC_Q)[:, None]
    c = np.arange(PAGE)[None, :]
    far = jnp.asarray(np.broadcast_to(past + t - (past - PAGE - 1), (DEC_Q, PAGE)), I32)
    last = jnp.asarray(t + PAGE - c, I32)
    new = jnp.asarray(t - c, I32)
    ok_new = jnp.asarray((c <= t) & (c < DEC_Q))
    tiles = []
    for rel, ok in ((far, None), (last, None), (new, ok_new)):
        b = jnp.transpose(_bucket_lookup(tab, rel), (2, 0, 1))
        if ok is not None:
            b = jnp.where(ok[None], b, NEG)
        tiles.append(jnp.repeat(b, groups_per_head, axis=0).reshape(-1, PAGE))
    return jnp.stack(tiles, axis=0)


def _masked_queries(q, n_groups, width, scale):
    lane = np.arange(GROUP_W)[None, :] // width == np.arange(n_groups)[:, None]
    m = jnp.asarray(lane, F32)[None, :, None, :] * (q * scale)[:, None, :, :]
    return m.reshape(q.shape[0], n_groups * DEC_Q, GROUP_W).astype(BF16)


def _pad_new(a):
    return jnp.pad(jnp.swapaxes(a, 1, 2), ((0, 0), (0, 0), (0, PAGE - DEC_Q))).astype(BF16)


def _feature_major_pages(cache):
    d, n_pool = cache.shape[:2]
    return jnp.swapaxes(cache.reshape(d, n_pool, PAGE, -1), 2, 3)


SEL_CHUNK = 32


def _dsa_sel_body(pt_ref, wqi_ref, wcol_ref, kinew_ref, tri_ref, *rest, n_pages, pps, n_pad, n_sel):
    del pt_ref
    ki_refs = rest[:pps]
    sel_ref, key_ref = rest[pps:]
    g = pl.program_id(1)
    wqi = wqi_ref[...]
    wcol = wcol_ref[...]

    def score(kts):
        kt = kts[0] if len(kts) == 1 else jnp.concatenate(kts, axis=1)
        s = jnp.maximum(_dot(wqi, kt), 0.0) * jnp.concatenate([wcol] * len(kts), axis=1)
        out = s[0:DEC_Q]
        for e in range(1, IDX_HEADS):
            out = out + s[e * DEC_Q:(e + 1) * DEC_Q]
        return out

    keys = _sort_key(score([ki_refs[i][...].astype(BF16) for i in range(pps)]))
    for i in range(pps):
        rows = pl.ds(pl.multiple_of((g * pps + i) * DEC_Q, DEC_Q), DEC_Q)
        key_ref[rows, :] = keys[:, i * PAGE:(i + 1) * PAGE]

    @pl.when(g == n_pages // pps - 1)
    def _():
        t = lax.broadcasted_iota(I32, (DEC_Q, PAGE), 0)
        c = lax.broadcasted_iota(I32, (DEC_Q, PAGE), 1)
        key_ref[n_pages * DEC_Q:(n_pages + 1) * DEC_Q, :] = jnp.where(c <= t, _sort_key(score([kinew_ref[...]])), INT_MIN)
        if n_pad > n_pages + 1:
            key_ref[(n_pages + 1) * DEC_Q:, :] = jnp.full(((n_pad - n_pages - 1) * DEC_Q, PAGE), INT_MIN, I32)

        def count(pred):
            tot = jnp.zeros((DEC_Q, PAGE), F32)
            for ch in range(n_pad // SEL_CHUNK):
                hit = jnp.where(pred(key_ref[ch * SEL_CHUNK * DEC_Q:(ch + 1) * SEL_CHUNK * DEC_Q, :]), 1.0, 0.0)
                for k in range(SEL_CHUNK):
                    tot = tot + hit[k * DEC_Q:(k + 1) * DEC_Q]
            return jnp.sum(tot, axis=-1, keepdims=True)

        tile_rows = lambda x: jnp.tile(jnp.broadcast_to(x, (DEC_Q, PAGE)), (SEL_CHUNK, 1))
        thr = _kth_largest(lambda cand: count(lambda k, cb=tile_rows(cand): k >= cb), float(n_sel), (DEC_Q, 1))
        n_ge = count(lambda k, tb=tile_rows(thr): k >= tb)
        no_surplus = jnp.max(n_ge) <= float(n_sel)

        @pl.when(no_surplus)
        def _():
            thr_b = jnp.broadcast_to(thr, (DEC_Q, PAGE))
            for j in range(n_pages + 1):
                sel_ref[j] = jnp.where(key_ref[j * DEC_Q:(j + 1) * DEC_Q, :] >= thr_b, 0.0, NEG)

        @pl.when(jnp.logical_not(no_surplus))
        def _():
            room = float(n_sel) - count(lambda k, tb=tile_rows(thr): k > tb)

            def select(j, seen):
                k = key_ref[pl.ds(pl.multiple_of(j * DEC_Q, DEC_Q), DEC_Q), :]
                eq = jnp.where(k == thr, 1.0, 0.0)
                rank = seen + _dot(eq.astype(BF16), tri_ref[...])
                keep = (k > thr) | ((k == thr) & (rank <= room))
                sel_ref[j] = jnp.where(keep, 0.0, NEG)
                return seen + jnp.sum(eq, axis=-1, keepdims=True)

            lax.fori_loop(0, n_pages + 1, select, jnp.zeros((DEC_Q, 1), F32))


def _dsa_decode_select(page_table, wqi, wcol, ki_new, cache_ki, l):
    nreq, n_pages = page_table.shape
    n_sel = min(TOPK_MAX, (n_pages * PAGE + DEC_Q) // 4)
    n_pad = -(-(n_pages + 1) // SEL_CHUNK) * SEL_CHUNK
    tri = jnp.asarray(np.arange(PAGE)[:, None] <= np.arange(PAGE)[None, :], BF16)
    rows = IDX_HEADS * DEC_Q
    pps = min(PPS_SELECT, n_pages)
    page_spec = lambda i: pl.BlockSpec((None, None, IDX_DIM, PAGE),
                                       lambda r, g, pt, i=i: (l, pt[r * n_pages + g * pps + i], 0, 0))
    cache_ki = _feature_major_pages(cache_ki)
    return pl.pallas_call(
        functools.partial(_dsa_sel_body, n_pages=n_pages, pps=pps, n_pad=n_pad, n_sel=n_sel),
        grid_spec=pltpu.PrefetchScalarGridSpec(
            num_scalar_prefetch=1,
            grid=(nreq, n_pages // pps),
            in_specs=[pl.BlockSpec((None, rows, IDX_DIM), lambda r, g, pt: (r, 0, 0)),
                      pl.BlockSpec((None, rows, PAGE), lambda r, g, pt: (r, 0, 0)),
                      pl.BlockSpec((None, IDX_DIM, PAGE), lambda r, g, pt: (r, 0, 0)),
                      pl.BlockSpec((PAGE, PAGE), lambda r, g, pt: (0, 0))]
            + [page_spec(i) for i in range(pps)],
            out_specs=pl.BlockSpec((None, n_pages + 1, DEC_Q, PAGE), lambda r, g, pt: (r, 0, 0, 0)),
            scratch_shapes=[pltpu.VMEM((n_pad * DEC_Q, PAGE), I32)]),
        out_shape=jax.ShapeDtypeStruct((nreq, n_pages + 1, DEC_Q, PAGE), F32),
        compiler_params=_cparams(("parallel", "arbitrary")),
        name="dsa_select",
    )(page_table.reshape(-1), wqi, wcol, ki_new, tri, *([cache_ki] * pps))


def _layer_params(l, prm):
    f32 = F32
    lb_all = jnp.cumsum(jax.nn.softmax(prm['hgrn_lb_logits'].astype(f32), axis=0), axis=0)
    lb = lb_all[l] - lb_all[0]
    lam_init = 0.8 - 0.6 * math.exp(-0.3 * l)
    lam = (jnp.exp(jnp.sum(prm['lq1'][l] * prm['lk1'][l])) - jnp.exp(jnp.sum(prm['lq2'][l] * prm['lk2'][l])) + lam_init)
    w_in = jnp.pad(prm['w_in'][l], ((0, 0), (0, D_IN_PAD - D_IN))).astype(BF16)
    return dict(
        w_in=w_in, log_lb=jnp.maximum(jnp.log(lb), NEG), log1m_lb=jnp.log1p(-lb), lam=lam.astype(f32), lam_init=lam_init,
        bias_b=_bias_tiles(prm['rel_bias'][:, :N_HEADS]), bias_d=_bias_tiles(prm['rel_bias'][:, N_HEADS:]),
        w_out=prm['w_out'][l].astype(BF16), w_gu=prm['w_gate_up'][l].astype(BF16), w_down=prm['w_down'][l].astype(BF16))


def _layer_prompt(x2d, bsz, seq, l, prm, lp, final):
    n = bsz * seq
    p = _proj(x2d, prm['attn_norm'][l], lp['w_in'])
    cols = lambda name: p[:, GROUP_W * COL[name]:GROUP_W * (COL[name] + 1)]
    b_k, b_v, d_k, d_v = cols('b_k'), cols('b_v'), cols('d_k'), cols('d_v')
    d_ki = p[:, 128 * TAIL_COL128:128 * TAIL_COL128 + IDX_DIM]

    st0 = jnp.zeros((bsz, GROUP_W, GROUP_W), F32)
    y_a, st = _hgrn(p, bsz, seq, lp['log_lb'], lp['log1m_lb'], prm['hgrn_norm'][l], st0, seq)

    y_b = _diff_prompt(p, bsz, seq, b_k.astype(BF16).reshape(bsz, seq, GROUP_W), _value_blocks_t(b_v, bsz, seq, TK),
                       lp['lam'], lp['bias_b'], prm['diff_norm'][l], 1.0 - lp['lam_init'])

    mix, bias_tile = _sgu_params(prm['sgu_w'][l], prm['sgu_b'][l], seq)
    y_c, c_v = _sgu(p, prm['sgu_norm'][l], mix, bias_tile)

    ki_rep = jnp.tile(d_ki, (1, IDX_HEADS)).astype(BF16).reshape(bsz, seq, GROUP_W)
    w_t = jnp.transpose(p[:, 128 * TAIL_COL128 + IDX_DIM:128 * TAIL_COL128 + IDX_DIM + IDX_HEADS])
    y_d = _dsa_prompt(p, bsz, seq, d_k.astype(BF16).reshape(bsz, seq, GROUP_W), _value_blocks_t(d_v, bsz, seq, TK),
                      ki_rep, w_t, lp['bias_d'])

    x_new = _post(x2d, (y_a, y_b, y_c, y_d), prm['mix_scale'][l], lp['w_out'], prm['ffn_norm'][l], lp['w_gu'],
                  lp['w_down'], prm['final_norm'], final)
    hd = lambda a: a.reshape(bsz, seq, N_HEADS, HEAD_DV)
    return x_new, (hd(b_k), hd(b_v), hd(d_k), hd(d_v), d_ki.reshape(bsz, seq, IDX_DIM), _blockdiag_to_state(st),
                   c_v.reshape(bsz, seq, GROUP_W))


def _layer_decode(x2d, nreq, l, prm, lp, past, final):
    n = nreq * DEC_Q
    pt = past['page_table']
    past_len = pt.shape[1] * PAGE
    p = _proj(x2d, prm['attn_norm'][l], lp['w_in'])
    cols = lambda name: p[:, GROUP_W * COL[name]:GROUP_W * (COL[name] + 1)]
    per_req = lambda a: a.reshape(nreq, DEC_Q, a.shape[-1])
    b_k, b_v, d_k, d_v = cols('b_k'), cols('b_v'), cols('d_k'), cols('d_v')
    tail = p[:, 128 * TAIL_COL128:128 * (TAIL_COL128 + 1)]
    d_ki = tail[:, :IDX_DIM]

    p16 = jnp.pad(per_req(p), ((0, 0), (0, HG - DEC_Q), (0, 0))).reshape(nreq * HG, D_IN_PAD)
    y_a16, st = _hgrn(p16, nreq, HG, lp['log_lb'], lp['log1m_lb'], prm['hgrn_norm'][l],
                      _state_to_blockdiag(past['hgrn'][l]), DEC_Q)
    y_a = y_a16.reshape(nreq, HG, GROUP_W)[:, :DEC_Q].reshape(n, GROUP_W)

    mix, bias_tile = _sgu_params(prm['sgu_w'][l], prm['sgu_b'][l], DEC_Q)
    y_c, c_v = _sgu(p, prm['sgu_norm'][l], mix, bias_tile)

    y_b = _paged_attn(pt, lp['lam'], _masked_queries(per_req(cols('b_q')), 2 * N_HEADS, DIFF_DQ, DIFF_DQ ** -0.5),
                      _decode_add_tiles(prm['rel_bias'][:, :N_HEADS], past_len, 2), _pad_new(per_req(b_k)),
                      _pad_new(per_req(b_v)), jnp.tile(prm['diff_norm'][l], N_HEADS), None,
                      past['k_diff'], past['v_diff'], l, diff=True, out_scale=1.0 - lp['lam_init'])

    qi = per_req(cols('d_qi')).reshape(nreq, DEC_Q, IDX_HEADS, IDX_DIM)
    wqi = jnp.swapaxes(qi, 1, 2).reshape(nreq, IDX_HEADS * DEC_Q, IDX_DIM).astype(BF16)
    w = per_req(tail[:, IDX_DIM:IDX_DIM + IDX_HEADS] * (IDX_HEADS ** -0.5 * IDX_DIM ** -0.5))
    wcol = jnp.broadcast_to(jnp.swapaxes(w, 1, 2).reshape(nreq, IDX_HEADS * DEC_Q, 1), (nreq, IDX_HEADS * DEC_Q, PAGE))
    sel = _dsa_decode_select(pt, wqi, wcol, _pad_new(per_req(d_ki)), past['k_index'], l)
    y_d = _paged_attn(pt, lp['lam'], _masked_queries(per_req(cols('d_q')), N_HEADS, HEAD_DV, HEAD_DV ** -0.5),
                      _decode_add_tiles(prm['rel_bias'][:, N_HEADS:], past_len, 1), _pad_new(per_req(d_k)),
                      _pad_new(per_req(d_v)), jnp.ones((GROUP_W,), F32), sel,
                      past['k_sparse'], past['v_sparse'], l, diff=False, out_scale=1.0)

    x_new = _post(x2d, (y_a, y_b, y_c, y_d), prm['mix_scale'][l], lp['w_out'], prm['ffn_norm'][l], lp['w_gu'],
                  lp['w_down'], prm['final_norm'], final)
    hd = lambda a: a.reshape(nreq, DEC_Q, N_HEADS, HEAD_DV)
    return x_new, (hd(b_k), hd(b_v), hd(d_k), hd(d_v), d_ki.reshape(nreq, DEC_Q, IDX_DIM), _blockdiag_to_state(st),
                   c_v.reshape(nreq, DEC_Q, GROUP_W))


def kernel(x_prompt, x_sample, cache_k_diff, cache_v_diff, cache_k_sparse, cache_v_sparse, cache_k_index, state_hgrn, page_table, w_in, w_out, attn_norm, ffn_norm, final_norm, w_gate_up, w_down, hgrn_lb_logits, hgrn_norm, diff_lambda_q1, diff_lambda_k1, diff_lambda_q2, diff_lambda_k2, diff_norm, sgu_w, sgu_b, sgu_norm, mix_scale, rel_bias):
    prm = dict(w_in=w_in, w_out=w_out, attn_norm=attn_norm, ffn_norm=ffn_norm, final_norm=final_norm,
               w_gate_up=w_gate_up, w_down=w_down, hgrn_lb_logits=hgrn_lb_logits, hgrn_norm=hgrn_norm,
               lq1=diff_lambda_q1, lk1=diff_lambda_k1, lq2=diff_lambda_q2, lk2=diff_lambda_k2, diff_norm=diff_norm,
               sgu_w=sgu_w, sgu_b=sgu_b, sgu_norm=sgu_norm, mix_scale=mix_scale, rel_bias=rel_bias)
    past = dict(page_table=page_table, k_diff=cache_k_diff, v_diff=cache_v_diff, k_sparse=cache_k_sparse,
                v_sparse=cache_v_sparse, k_index=cache_k_index, hgrn=state_hgrn)
    depth = w_in.shape[0]
    bsz, seq, d = x_prompt.shape
    nreq = x_sample.shape[0]
    xp = x_prompt.reshape(bsz * seq, d)
    xs = x_sample.reshape(nreq * DEC_Q, d)
    st_p, st_s = [], []
    for l in range(depth):
        lp = _layer_params(l, prm)
        final = l == depth - 1
        xp, sp = _layer_prompt(xp, bsz, seq, l, prm, lp, final)
        xs, ss = _layer_decode(xs, nreq, l, prm, lp, past, final)
        st_p.append(sp)
        st_s.append(ss)
    stk = lambda lst, i: jnp.stack([s[i] for s in lst], axis=0)
    return (xp.reshape(bsz, seq, d), xs.reshape(nreq, DEC_Q, d),
            stk(st_p, 0), stk(st_p, 1), stk(st_p, 2), stk(st_p, 3), stk(st_p, 4), stk(st_p, 5),
            stk(st_s, 0), stk(st_s, 1), stk(st_s, 2), stk(st_s, 3), stk(st_s, 4), stk(st_s, 5), stk(st_s, 6))
```

```python
import functools
import math

import numpy as np
import jax
import jax.numpy as jnp
from jax import lax
from jax.experimental import pallas as pl
from jax.experimental.pallas import tpu as pltpu

F32 = jnp.float32
BF16 = jnp.bfloat16
I32 = jnp.int32

D_MODEL = 1024
GROUP_W = 256
N_HEADS = 4
HEAD_DV = 64
DIFF_DQ = 32
IDX_HEADS = 8
IDX_DIM = 32
TOPK_MAX = 256
REL_BUCKETS = 32
REL_EXACT = 16
REL_MAX_DIST = 128
PAGE = 128
D_FF = 2816
EPS = 1e-6
D_IN = 3368
D_IN_PAD = 3456
COL = {'a_q': 0, 'a_f': 1, 'a_i': 2, 'a_g': 3, 'b_q': 4, 'b_k': 5, 'b_v': 6, 'c_u': 7, 'c_v': 8,
       'd_q': 9, 'd_k': 10, 'd_v': 11, 'd_qi': 12}
TAIL_COL128 = 26
NEG = -1e30
INT_MIN = -2 ** 31
TQ = 128
HG = 16
HGRN_PAIR = 4
HGRN_ROWS = 512
VMEM_LIMIT = 56 * 1024 * 1024


def _cparams(sem):
    return pltpu.CompilerParams(dimension_semantics=sem, vmem_limit_bytes=VMEM_LIMIT)


def _dot(a, b):
    return jnp.dot(a, b, preferred_element_type=F32)


def _dot_nt(a, b):
    return lax.dot_general(a, b, (((1,), (1,)), ((), ())), preferred_element_type=F32)


def _dot_tn(a, b):
    return lax.dot_general(a, b, (((0,), (0,)), ((), ())), preferred_element_type=F32)


def _sigmoid(x):
    return 1.0 / (1.0 + jnp.exp(-x))


def _split3(x):
    hi = x.astype(BF16)
    r = x - hi.astype(F32)
    mid = r.astype(BF16)
    lo = (r - mid.astype(F32)).astype(BF16)
    return hi, mid, lo


def _dot_exact_lhs(m_bf16, x):
    hi, mid, lo = _split3(x)
    return _dot(m_bf16, hi) + _dot(m_bf16, mid) + _dot(m_bf16, lo)


def _dot_exact_rhs(x, m_bf16):
    hi, mid, lo = _split3(x)
    return _dot(hi, m_bf16) + _dot(mid, m_bf16) + _dot(lo, m_bf16)


def _proj_body(x_ref, g_ref, w_ref, o_ref):
    x = x_ref[...]
    h = x * lax.rsqrt(jnp.mean(x * x, axis=-1, keepdims=True) + EPS) * g_ref[...]
    o_ref[...] = _dot(h.astype(BF16), w_ref[...])


def _proj(x2d, gain, w_bf16):
    n, d = x2d.shape
    dn = w_bf16.shape[1]
    tm = min(256, n)
    return pl.pallas_call(
        _proj_body,
        grid=(n // tm,),
        in_specs=[pl.BlockSpec((tm, d), lambda i: (i, 0)),
                  pl.BlockSpec((1, d), lambda i: (0, 0)),
                  pl.BlockSpec((d, dn), lambda i: (0, 0))],
        out_specs=pl.BlockSpec((tm, dn), lambda i: (i, 0)),
        out_shape=jax.ShapeDtypeStruct((n, dn), F32),
        compiler_params=_cparams(("parallel",)),
        name="proj",
    )(x2d, gain.reshape(1, d), w_bf16)


FF_CHUNK = 256


def _outproj_body(x_ref, ya_ref, yb_ref, yc_ref, yd_ref, ms_ref, wo_ref, o_ref):
    acc = x_ref[...]
    for i, y_ref in enumerate((ya_ref, yb_ref, yc_ref, yd_ref)):
        lo, hi = GROUP_W * i, GROUP_W * (i + 1)
        yi = (y_ref[...] * ms_ref[:, lo:hi]).astype(BF16)
        acc = acc + _dot(yi, wo_ref[lo:hi, :])
    o_ref[...] = acc


def _ffn_body(x_ref, fn_ref, wgu_ref, wd_ref, fin_ref, o_ref, acc_ref, *, final):
    x = x_ref[...]
    acc_ref[...] = x
    h = (x * lax.rsqrt(jnp.mean(x * x, axis=-1, keepdims=True) + EPS) * fn_ref[...]).astype(BF16)
    for c in range(D_FF // FF_CHUNK):
        lo, hi = FF_CHUNK * c, FF_CHUNK * (c + 1)
        g = _dot(h, wgu_ref[:, lo:hi])
        u = _dot(h, wgu_ref[:, D_FF + lo:D_FF + hi])
        a = (g * _sigmoid(g) * u).astype(BF16)
        acc_ref[...] += _dot(a, wd_ref[lo:hi, :])
    out = acc_ref[...]
    if final:
        out = out * lax.rsqrt(jnp.mean(out * out, axis=-1, keepdims=True) + EPS) * fin_ref[...]
    o_ref[...] = out


def _post(x2d, ys, mix_scale, wo, ffn_norm, wgu, wd, final_norm, final):
    n, d = x2d.shape
    tm = min(256, n)
    row = lambda i: (i, 0)
    const = lambda i: (0, 0)
    x1 = pl.pallas_call(
        _outproj_body,
        grid=(n // tm,),
        in_specs=[pl.BlockSpec((tm, d), row)]
        + [pl.BlockSpec((tm, GROUP_W), row) for _ in range(4)]
        + [pl.BlockSpec((1, d), const), pl.BlockSpec((d, d), const)],
        out_specs=pl.BlockSpec((tm, d), row),
        out_shape=jax.ShapeDtypeStruct((n, d), F32),
        compiler_params=_cparams(("parallel",)),
        name="outproj",
    )(x2d, *ys, mix_scale.reshape(1, d), wo)
    return pl.pallas_call(
        functools.partial(_ffn_body, final=final),
        grid=(n // tm,),
        in_specs=[pl.BlockSpec((tm, d), row),
                  pl.BlockSpec((1, d), const),
                  pl.BlockSpec((d, 2 * D_FF), const),
                  pl.BlockSpec((D_FF, d), const),
                  pl.BlockSpec((1, d), const)],
        out_specs=pl.BlockSpec((tm, d), row),
        out_shape=jax.ShapeDtypeStruct((n, d), F32),
        scratch_shapes=[pltpu.VMEM((tm, d), F32)],
        compiler_params=_cparams(("parallel",)),
        name="ffn",
    )(x1, ffn_norm.reshape(1, d), wgu, wd, final_norm.reshape(1, d))


def _hgrn_consts(tr):
    r = np.arange(tr)
    same = (r[:, None] // HG) == (r[None, :] // HG)
    tri = same & (r[None, :] <= r[:, None])
    c = np.arange(GROUP_W)
    heads = (c[:, None] // HEAD_DV) == (c[None, :] // HEAD_DV)
    return (jnp.asarray(tri, BF16), jnp.asarray(same, BF16), jnp.asarray(heads, BF16), jnp.asarray(heads, F32))


def _shift_rows(x, d, row):
    if d == 0:
        return x
    return jnp.where(row >= d, pltpu.roll(x, d, axis=0), 0.0)


def _hgrn_body(aq_ref, af_ref, ai_ref, ag_ref, la_ref, l1_ref, hn_ref, tri_ref, same_ref, e_ref, mask_ref, st0_ref,
               ya_ref, st_ref, cum_ref, tot_ref, kk_ref, qq_ref, qe_ref, ke_ref, o_ref, *, sc, tr, valid_rows, seq):
    c = pl.program_id(1)

    @pl.when(c == 0)
    def _():
        st_ref[...] = st0_ref[...]

    def prep(t, carry):
        r0 = pl.multiple_of(t * tr, tr)
        rows = pl.ds(r0, tr)
        for e in range(HGRN_PAIR):
            z = af_ref[e, rows, :]
            ls = jnp.minimum(z, 0.0) - jnp.log(1.0 + jnp.exp(-jnp.abs(z)))
            b = l1_ref[...] + ls
            a = la_ref[...]
            lf = jnp.maximum(a, b) + jnp.log(1.0 + jnp.exp(-jnp.abs(a - b)))
            kk = 1.0 - jnp.exp(lf)
            if valid_rows < seq:
                ok = (lax.broadcasted_iota(I32, (tr, GROUP_W), 0) + r0 + c * sc) < valid_rows
                lf = jnp.where(ok, lf, 0.0)
                kk = jnp.where(ok, kk, 0.0)
            cum = _dot_exact_lhs(tri_ref[...], lf)
            tot = _dot_exact_lhs(same_ref[...], lf)
            q = aq_ref[e, rows, :]
            qq = q * _sigmoid(q)
            cum_ref[e, rows, :] = cum
            tot_ref[e, rows, :] = tot
            kk_ref[e, rows, :] = kk
            qq_ref[e, rows, :] = qq
            qe_ref[e, rows, :] = (qq * jnp.exp(cum)).astype(BF16)
            ke_ref[e, rows, :] = (kk * jnp.exp(tot - cum)).astype(BF16)
        return carry

    lax.fori_loop(0, sc // tr, prep, 0)

    row = lax.broadcasted_iota(I32, (HG, GROUP_W), 0)

    def block(i, carry):
        rows = pl.ds(pl.multiple_of(i * HG, HG), HG)
        for e in range(HGRN_PAIR):
            cum = cum_ref[e, rows, :]
            kk = kk_ref[e, rows, :]
            qq = qq_ref[e, rows, :]
            v = ai_ref[e, rows, :]
            o = _dot_nt(qe_ref[e, rows, :], st_ref[e].astype(BF16))
            for d in range(HG):
                x = qq * _shift_rows(kk, d, row) * jnp.exp(cum - _shift_rows(cum, d, row))
                o = o + _dot(x.astype(BF16), e_ref[...]) * _shift_rows(v, d, row)
            o_ref[e, rows, :] = o
            dec = jnp.exp(tot_ref[e, pl.ds(pl.multiple_of(i * HG, HG), 1), :])
            upd = _dot_tn(v.astype(BF16), ke_ref[e, rows, :])
            st_ref[e] = st_ref[e] * dec + mask_ref[...] * upd
        return carry

    lax.fori_loop(0, sc // HG, block, 0)

    def fin(t, carry):
        rows = pl.ds(pl.multiple_of(t * tr, tr), tr)
        for e in range(HGRN_PAIR):
            o = o_ref[e, rows, :]
            hi = (o * o).astype(BF16)
            lo = (o * o - hi.astype(F32)).astype(BF16)
            msq = (_dot(hi, e_ref[...]) + _dot(lo, e_ref[...])) * (1.0 / HEAD_DV)
            g = ag_ref[e, rows, :]
            ya_ref[e, rows, :] = o * lax.rsqrt(msq + EPS) * hn_ref[...] * (g * _sigmoid(g))
        return carry

    lax.fori_loop(0, sc // tr, fin, 0)


def _hgrn(p2d, bsz, seq, log_lb, log1m_lb, hnorm, st0, valid_rows):
    assert bsz % HGRN_PAIR == 0
    sc = min(HGRN_ROWS, seq)
    tr = min(128, sc)
    tri, same, e_bf, mask = _hgrn_consts(tr)
    col = lambda name: pl.BlockSpec((HGRN_PAIR, sc, GROUP_W), lambda b, c, k=COL[name]: (b, c, k))
    vec = pl.BlockSpec((1, GROUP_W), lambda b, c: (0, 0))
    mat = lambda n: pl.BlockSpec((n, n), lambda b, c: (0, 0))
    st_spec = pl.BlockSpec((HGRN_PAIR, GROUP_W, GROUP_W), lambda b, c: (b, 0, 0))
    big = lambda dt: pltpu.VMEM((HGRN_PAIR, sc, GROUP_W), dt)
    p3d = p2d.reshape(bsz, seq, p2d.shape[1])
    ya, st = pl.pallas_call(
        functools.partial(_hgrn_body, sc=sc, tr=tr, valid_rows=valid_rows, seq=seq),
        grid=(bsz // HGRN_PAIR, seq // sc),
        in_specs=[col('a_q'), col('a_f'), col('a_i'), col('a_g'), vec, vec, vec, mat(tr), mat(tr),
                  mat(GROUP_W), mat(GROUP_W), st_spec],
        out_specs=[pl.BlockSpec((HGRN_PAIR, sc, GROUP_W), lambda b, c: (b, c, 0)), st_spec],
        out_shape=[jax.ShapeDtypeStruct((bsz, seq, GROUP_W), F32),
                   jax.ShapeDtypeStruct((bsz, GROUP_W, GROUP_W), F32)],
        scratch_shapes=[big(F32), big(F32), big(F32), big(F32), big(BF16), big(BF16), big(F32)],
        compiler_params=_cparams(("parallel", "arbitrary")),
        name="hgrn",
    )(p3d, p3d, p3d, p3d, log_lb.reshape(1, GROUP_W), log1m_lb.reshape(1, GROUP_W), hnorm.reshape(1, GROUP_W),
      tri, same, e_bf, mask, st0)
    return ya.reshape(bsz * seq, GROUP_W), st


def _state_to_blockdiag(s):
    st = jnp.swapaxes(s, 2, 3)
    eye = jnp.eye(N_HEADS, dtype=s.dtype)
    return jnp.einsum('bhvk,hg->bhvgk', st, eye).reshape(s.shape[0], GROUP_W, GROUP_W)


def _blockdiag_to_state(st):
    b = st.shape[0]
    st5 = st.reshape(b, N_HEADS, HEAD_DV, N_HEADS, HEAD_DV)
    diag = jnp.stack([st5[:, h, :, h, :] for h in range(N_HEADS)], axis=1)
    return jnp.swapaxes(diag, 2, 3)


def _gelu(x):
    return 0.5 * x * (1.0 + lax.erf(x * (2.0 ** -0.5)))


def _sgu_body(cu_ref, cv_ref, sn_ref, mix_ref, bias_ref, yc_ref, cvn_ref, *, tr):
    u = _gelu(cu_ref[...])
    v = _gelu(cv_ref[...])
    vn = v * lax.rsqrt(jnp.mean(v * v, axis=-1, keepdims=True) + EPS) * sn_ref[...]
    cvn_ref[...] = vn
    vb = vn.astype(BF16)
    lane = lax.broadcasted_iota(I32, (tr, GROUP_W), 1)
    mixed = bias_ref[...]
    for g in range(N_HEADS):
        mg = _dot(mix_ref[g], vb)
        mixed = mixed + jnp.where((lane >= HEAD_DV * g) & (lane < HEAD_DV * (g + 1)), mg, 0.0)
    yc_ref[...] = u * mixed


def _sgu(p2d, snorm, mix_bf16, bias_tile):
    n = p2d.shape[0]
    tr = 128
    col = lambda name: pl.BlockSpec((tr, GROUP_W), lambda i, c=COL[name]: (i, c))
    out = pl.BlockSpec((tr, GROUP_W), lambda i: (i, 0))
    return pl.pallas_call(
        functools.partial(_sgu_body, tr=tr),
        grid=(n // tr,),
        in_specs=[col('c_u'), col('c_v'),
                  pl.BlockSpec((1, GROUP_W), lambda i: (0, 0)),
                  pl.BlockSpec((N_HEADS, tr, tr), lambda i: (0, 0, 0)),
                  pl.BlockSpec((tr, GROUP_W), lambda i: (0, 0))],
        out_specs=[out, out],
        out_shape=[jax.ShapeDtypeStruct((n, GROUP_W), F32)] * 2,
        compiler_params=_cparams(("parallel",)),
        name="sgu",
    )(p2d, p2d, snorm.reshape(1, GROUP_W), mix_bf16, bias_tile)


def _sgu_params(w, b, seq):
    c = min(128, seq)
    wt = jnp.tril(w[:, :c, :c])
    mix = jnp.einsum('rs,gtu->grtsu', jnp.eye(128 // c, dtype=w.dtype), wt).reshape(N_HEADS, 128, 128)
    bt = jnp.tile(jnp.transpose(b[:, :c]), (128 // c, 1))
    return mix.astype(BF16), jnp.repeat(bt, HEAD_DV, axis=1)


def _rel_bucket(rel):
    n = jnp.maximum(rel, 0)
    nf = jnp.maximum(n, 1).astype(F32)
    large = REL_EXACT + (jnp.log(nf / REL_EXACT) / math.log(REL_MAX_DIST / REL_EXACT)
                         * (REL_BUCKETS - REL_EXACT)).astype(I32)
    large = jnp.minimum(large, REL_BUCKETS - 1)
    return jnp.where(n < REL_EXACT, n, large)


def _bucket_lookup(tab, rel):
    onehot = (_rel_bucket(rel)[..., None] == jnp.arange(REL_BUCKETS, dtype=I32)).astype(F32)
    return jnp.einsum('...b,bh->...h', onehot, tab.astype(F32), precision=lax.Precision.HIGHEST)


def _bias_tiles(tab):
    assert REL_MAX_DIST <= TQ
    q = np.arange(TQ)[:, None]
    k = np.arange(TQ)[None, :]
    tiles = []
    for back in range(3):
        rel = jnp.asarray(q - k + TQ * back, I32)
        t = jnp.transpose(_bucket_lookup(tab, rel), (2, 0, 1))
        if back == 0:
            t = jnp.where(jnp.asarray(k <= q)[None], t, NEG)
        tiles.append(t)
    return jnp.stack(tiles, axis=0)


def _lane_band(x, lo, width):
    lane = lax.broadcasted_iota(I32, x.shape, x.ndim - 1)
    return jnp.where((lane >= lo) & (lane < lo + width), x, 0.0)


TK = 2 * TQ
COUNT_ROWS = 32


def _value_blocks_t(v2d, bsz, seq, tk):
    v = jnp.transpose(v2d.reshape(bsz, seq // tk, tk, N_HEADS, HEAD_DV), (0, 1, 3, 4, 2))
    return jnp.concatenate([v, jnp.ones_like(v)], axis=3).astype(BF16)


def _bias_tiles_t(bias, maps):
    t = jnp.transpose(bias, (0, 3, 1, 2))[:, :, :, None, :]
    t = jnp.broadcast_to(t, (3, TQ, N_HEADS, maps, TQ)).reshape(3, TQ, N_HEADS * maps * TQ)
    return jnp.concatenate([t, jnp.full_like(t[:1], NEG)], axis=0)


def _step_bias(bias_ref, back):
    first = bias_ref[jnp.where(back < 0, 3, jnp.minimum(back, 2))]
    second = bias_ref[jnp.where(back < 1, 3, jnp.minimum(back - 1, 2))]
    return jnp.concatenate([first, second], axis=0)


def _diff_body(lam_ref, q_ref, k_ref, vt_ref, bias_ref, gain_ref, o_ref, qm_ref, m_ref, acc_ref, *, out_scale):
    i = pl.program_id(1)
    q = q_ref[...] * (DIFF_DQ ** -0.5)
    for hc in range(2 * N_HEADS):
        qm_ref[hc * TQ:(hc + 1) * TQ, :] = _lane_band(q, DIFF_DQ * hc, DIFF_DQ).astype(BF16)
    m_ref[...] = jnp.full(m_ref.shape, NEG, F32)
    acc_ref[...] = jnp.zeros(acc_ref.shape, F32)

    n_step = (i + 2) // 2

    def step_pair(tt, carry):
        for u in range(2):
            t = 2 * tt + u
            tc = jnp.minimum(t, n_step - 1)
            s = (_dot_nt(k_ref[pl.ds(pl.multiple_of(tc * TK, TK), TK), :], qm_ref[...])
                 + _step_bias(bias_ref, jnp.where(t < n_step, i - 2 * t, -2)))
            m_old = m_ref[...]
            m_new = jnp.maximum(m_old, jnp.max(s, axis=0, keepdims=True))
            alpha = jnp.exp(m_old - m_new)
            p = jnp.exp(s - m_new).astype(BF16)
            for h in range(N_HEADS):
                cols = slice(2 * h * TQ, (2 * h + 2) * TQ)
                acc_ref[h] = alpha[:, cols] * acc_ref[h] + _dot(vt_ref[tc, h], p[:, cols])
            m_ref[...] = m_new
        return carry

    lax.fori_loop(0, (n_step + 1) // 2, step_pair, 0)

    lam = lam_ref[0]
    res = []
    for h in range(N_HEADS):
        a = acc_ref[h]
        n = a[:HEAD_DV, :] / a[HEAD_DV:HEAD_DV + 1, :]
        o = n[:, :TQ] - lam * n[:, TQ:]
        msq = jnp.mean(o * o, axis=0, keepdims=True)
        res.append(o * lax.rsqrt(msq + EPS) * gain_ref[...] * out_scale)
    o_ref[...] = jnp.transpose(jnp.concatenate(res, axis=0))


def _diff_prompt(p2d, bsz, seq, k_bf16, v_blocks_t, lam, bias, gain, out_scale):
    nq = seq // TQ
    assert nq % 2 == 0
    ncol = 2 * N_HEADS * TQ
    return pl.pallas_call(
        functools.partial(_diff_body, out_scale=out_scale),
        grid=(bsz, nq),
        in_specs=[pl.BlockSpec(memory_space=pltpu.SMEM),
                  pl.BlockSpec((TQ, GROUP_W), lambda b, i: (b * nq + i, COL['b_q'])),
                  pl.BlockSpec((None, seq, GROUP_W), lambda b, i: (b, 0, 0)),
                  pl.BlockSpec((None, seq // TK, N_HEADS, 2 * HEAD_DV, TK), lambda b, i: (b, 0, 0, 0, 0)),
                  pl.BlockSpec((4, TQ, ncol), lambda b, i: (0, 0, 0)),
                  pl.BlockSpec((HEAD_DV, TQ), lambda b, i: (0, 0))],
        out_specs=pl.BlockSpec((TQ, GROUP_W), lambda b, i: (b * nq + i, 0)),
        out_shape=jax.ShapeDtypeStruct((bsz * seq, GROUP_W), F32),
        scratch_shapes=[pltpu.VMEM((ncol, GROUP_W), BF16),
                        pltpu.VMEM((1, ncol), F32),
                        pltpu.VMEM((N_HEADS, 2 * HEAD_DV, 2 * TQ), F32)],
        compiler_params=_cparams(("parallel", "parallel")),
        name="diff_prompt",
    )(lam.reshape(1), p2d, k_bf16, v_blocks_t, _bias_tiles_t(bias, 2),
      jnp.broadcast_to(gain[:, None], (HEAD_DV, TQ)))


def _sort_key(score):
    bits = lax.bitcast_convert_type(score, I32)
    return jnp.where(bits < 0, bits ^ 0x7FFFFFFF, bits)


def _kth_largest(count_ge, n_sel, shape):
    t = jnp.full(shape, INT_MIN, I32)
    for bit in range(31, -1, -1):
        step = INT_MIN if bit == 31 else (1 << bit)
        cand = t + jnp.int32(step)
        t = jnp.where(count_ge(cand) >= n_sel, cand, t)
    return t


def _dsa_body(q_ref, qi_ref, wt_ref, k_ref, vt_ref, ki_ref, bias_ref, tri_ref, o_ref,
              qm_ref, qim_ref, key_ref, sel_ref, m_ref, acc_ref, *, n_sel):
    i = pl.program_id(1)
    q = q_ref[...] * (HEAD_DV ** -0.5)
    for h in range(N_HEADS):
        qm_ref[h * TQ:(h + 1) * TQ, :] = _lane_band(q, HEAD_DV * h, HEAD_DV).astype(BF16)
    qi = qi_ref[...]
    for e in range(IDX_HEADS):
        qim_ref[e * TQ:(e + 1) * TQ, :] = _lane_band(qi, IDX_DIM * e, IDX_DIM).astype(BF16)
    w = wt_ref[...] * (IDX_HEADS ** -0.5 * IDX_DIM ** -0.5)
    krow = lax.broadcasted_iota(I32, (TK, TQ), 0)
    qcol = lax.broadcasted_iota(I32, (TK, TQ), 1)
    n_step = (i + 2) // 2

    def score_pair(tt, carry):
        for u in range(2):
            t = jnp.minimum(2 * tt + u, n_step - 1)
            s = jnp.maximum(_dot_nt(ki_ref[pl.ds(pl.multiple_of(t * TK, TK), TK), :], qim_ref[...]), 0.0)
            sc = s[:, 0:TQ] * w[0:1, :]
            for e in range(1, IDX_HEADS):
                sc = sc + s[:, e * TQ:(e + 1) * TQ] * w[e:e + 1, :]
            causal = (krow + t * TK) <= (qcol + i * TQ)
            key_ref[t] = jnp.where(causal, _sort_key(sc), INT_MIN)
        return carry

    lax.fori_loop(0, (n_step + 1) // 2, score_pair, 0)

    def count(pred):
        def body(t, c):
            hit = jnp.where(pred(key_ref[t]), 1.0, 0.0)
            return c + jnp.sum(hit.reshape(TK // COUNT_ROWS, COUNT_ROWS, TQ), axis=0)
        return jnp.sum(lax.fori_loop(0, n_step, body, jnp.zeros((COUNT_ROWS, TQ), F32)), axis=0, keepdims=True)

    thr = _kth_largest(lambda cand: count(lambda k: k >= cand), float(n_sel), (1, TQ))
    no_surplus = jnp.max(count(lambda k: k >= thr)) <= float(n_sel)

    @pl.when(no_surplus)
    def _():
        def select(t, carry):
            sel_ref[t] = jnp.where(key_ref[t] >= thr, 0.0, NEG)
            return carry

        lax.fori_loop(0, n_step, select, 0)

    @pl.when(jnp.logical_not(no_surplus))
    def _():
        room = float(n_sel) - count(lambda k: k > thr)

        def select(t, seen):
            k = key_ref[t]
            eq = jnp.where(k == thr, 1.0, 0.0)
            rank = seen + _dot(tri_ref[...], eq.astype(BF16))
            keep = ((k > thr) | ((k == thr) & (rank <= room))) & (k != INT_MIN)
            sel_ref[t] = jnp.where(keep, 0.0, NEG)
            return seen + jnp.sum(eq, axis=0, keepdims=True)

        lax.fori_loop(0, n_step, select, jnp.zeros((1, TQ), F32))

    m_ref[...] = jnp.full(m_ref.shape, NEG, F32)
    acc_ref[...] = jnp.zeros(acc_ref.shape, F32)

    def attend_pair(tt, carry):
        for u in range(2):
            t = 2 * tt + u
            tc = jnp.minimum(t, n_step - 1)
            sel = sel_ref[tc]
            s = (_dot_nt(k_ref[pl.ds(pl.multiple_of(tc * TK, TK), TK), :], qm_ref[...])
                 + _step_bias(bias_ref, jnp.where(t < n_step, i - 2 * t, -2))
                 + jnp.concatenate([sel] * N_HEADS, axis=1))
            m_old = m_ref[...]
            m_new = jnp.maximum(m_old, jnp.max(s, axis=0, keepdims=True))
            alpha = jnp.exp(m_old - m_new)
            p = jnp.exp(s - m_new).astype(BF16)
            for h in range(N_HEADS):
                cols = slice(h * TQ, (h + 1) * TQ)
                acc_ref[h] = alpha[:, cols] * acc_ref[h] + _dot(vt_ref[tc, h], p[:, cols])
            m_ref[...] = m_new
        return carry

    lax.fori_loop(0, (n_step + 1) // 2, attend_pair, 0)
    out_t = jnp.concatenate([acc_ref[h][:HEAD_DV, :] / acc_ref[h][HEAD_DV:HEAD_DV + 1, :] for h in range(N_HEADS)],
                            axis=0)
    o_ref[...] = jnp.transpose(out_t)


def _dsa_prompt(p2d, bsz, seq, k_bf16, v_blocks_t, ki_rep, w_t, bias):
    nq = seq // TQ
    assert nq % 2 == 0
    n_sel = min(TOPK_MAX, seq // 4)
    tri = jnp.asarray(np.arange(TK)[None, :] <= np.arange(TK)[:, None], BF16)
    qcol = lambda c: pl.BlockSpec((TQ, GROUP_W), lambda b, i, c=c: (b * nq + i, c))
    full = pl.BlockSpec((None, seq, GROUP_W), lambda b, i: (b, 0, 0))
    return pl.pallas_call(
        functools.partial(_dsa_body, n_sel=n_sel),
        grid=(bsz, nq),
        in_specs=[qcol(COL['d_q']), qcol(COL['d_qi']),
                  pl.BlockSpec((IDX_HEADS, TQ), lambda b, i: (0, b * nq + i)),
                  full,
                  pl.BlockSpec((None, seq // TK, N_HEADS, 2 * HEAD_DV, TK), lambda b, i: (b, 0, 0, 0, 0)),
                  full,
                  pl.BlockSpec((4, TQ, N_HEADS * TQ), lambda b, i: (0, 0, 0)),
                  pl.BlockSpec((TK, TK), lambda b, i: (0, 0))],
        out_specs=pl.BlockSpec((TQ, GROUP_W), lambda b, i: (b * nq + i, 0)),
        out_shape=jax.ShapeDtypeStruct((bsz * seq, GROUP_W), F32),
        scratch_shapes=[pltpu.VMEM((N_HEADS * TQ, GROUP_W), BF16),
                        pltpu.VMEM((IDX_HEADS * TQ, GROUP_W), BF16),
                        pltpu.VMEM((seq // TK, TK, TQ), I32),
                        pltpu.VMEM((seq // TK, TK, TQ), F32),
                        pltpu.VMEM((1, N_HEADS * TQ), F32),
                        pltpu.VMEM((N_HEADS, 2 * HEAD_DV, TQ), F32)],
        compiler_params=_cparams(("parallel", "parallel")),
        name="dsa_prompt",
    )(p2d, p2d, w_t, k_bf16, v_blocks_t, ki_rep, _bias_tiles_t(bias, 1), tri)


DEC_Q = 8
PPS_ATTN = 32
PPS_SELECT = 32
PAGE_GROUP = 8


def _paged_update(s, vts, m_ref, l_ref, acc_ref):
    n = len(vts)
    m_old = m_ref[...]
    m_new = jnp.maximum(m_old, jnp.max(s, axis=-1, keepdims=True))
    alpha = jnp.exp(m_old - m_new)
    p = jnp.exp(s - jnp.concatenate([m_new] * n, axis=1))
    l = alpha * l_ref[...]
    acc = jnp.concatenate([alpha, alpha], axis=1) * acc_ref[...]
    for i in range(n):
        pi = p[:, i * PAGE:(i + 1) * PAGE]
        l = l + pi
        acc = acc + _dot_nt(pi.astype(BF16), vts[i])
    l_ref[...] = l
    acc_ref[...] = acc
    m_ref[...] = m_new


def _paged_body(pt_ref, lam_ref, wq_ref, add_ref, knew_ref, vnew_ref, gain_ref, e_ref, *rest,
                n_pages, pps, rows, diff, use_sel, out_scale):
    del pt_ref
    if use_sel:
        sel_ref, selnew_ref = rest[:2]
        rest = rest[2:]
    k_refs, v_refs = rest[:pps], rest[pps:2 * pps]
    o_ref, m_ref, l_ref, acc_ref = rest[2 * pps:]
    g = pl.program_id(1)

    @pl.when(g == 0)
    def _():
        m_ref[...] = jnp.full(m_ref.shape, NEG, F32)
        l_ref[...] = jnp.zeros(l_ref.shape, F32)
        acc_ref[...] = jnp.zeros(acc_ref.shape, F32)

    wq = wq_ref[...]
    last_step = g == n_pages // pps - 1
    far = add_ref[0]
    for lo in range(0, pps, PAGE_GROUP):
        parts = []
        for i in range(lo, min(lo + PAGE_GROUP, pps)):
            si = _dot(wq, k_refs[i][...].astype(BF16))
            si = si + (far if i < pps - 1 else jnp.where(last_step, add_ref[1], far))
            if use_sel:
                si = si + jnp.tile(sel_ref[i], (rows // DEC_Q, 1))
            parts.append(si)
        _paged_update(jnp.concatenate(parts, axis=1),
                      [v_refs[i][...].astype(BF16) for i in range(lo, min(lo + PAGE_GROUP, pps))],
                      m_ref, l_ref, acc_ref)

    @pl.when(last_step)
    def _():
        s = _dot(wq, knew_ref[...]) + add_ref[2]
        if use_sel:
            s = s + jnp.tile(selnew_ref[0], (rows // DEC_Q, 1))
        _paged_update(s, [vnew_ref[...]], m_ref, l_ref, acc_ref)
        o_all = acc_ref[...] / jnp.sum(l_ref[...], axis=-1, keepdims=True)
        out = jnp.zeros((DEC_Q, GROUP_W), F32)
        for h in range(N_HEADS):
            if diff:
                lo = 2 * h * DEC_Q
                o_h = o_all[lo:lo + DEC_Q] - lam_ref[0] * o_all[lo + DEC_Q:lo + 2 * DEC_Q]
            else:
                o_h = o_all[h * DEC_Q:(h + 1) * DEC_Q]
            out = out + _lane_band(o_h, HEAD_DV * h, HEAD_DV)
        if diff:
            sq = out * out
            hi = sq.astype(BF16)
            lo_ = (sq - hi.astype(F32)).astype(BF16)
            msq = (_dot(hi, e_ref[...]) + _dot(lo_, e_ref[...])) * (1.0 / HEAD_DV)
            out = out * lax.rsqrt(msq + EPS) * gain_ref[...] * out_scale
        o_ref[...] = out


def _paged_attn(page_table, lam, wq, add, k_new, v_new, gain256, sel, cache_k, cache_v, l, *, diff, out_scale):
    nreq, n_pages = page_table.shape
    rows = wq.shape[1]
    n_pool = cache_k.shape[1]
    ck = _feature_major_pages(cache_k)
    cv = _feature_major_pages(cache_v)
    use_sel = sel is not None
    pps = min(PPS_ATTN, n_pages)
    page_spec = lambda i: pl.BlockSpec((None, None, GROUP_W, PAGE),
                                       lambda r, g, pt, i=i: (l, pt[r * n_pages + g * pps + i], 0, 0))
    per_req = lambda shape: pl.BlockSpec((None,) + shape, lambda r, g, pt: (r,) + (0,) * len(shape))
    const = lambda shape: pl.BlockSpec(shape, lambda r, g, pt: (0,) * len(shape))
    in_specs = [pl.BlockSpec(memory_space=pltpu.SMEM), per_req((rows, GROUP_W)), const((3, rows, PAGE)),
                per_req((GROUP_W, PAGE)), per_req((GROUP_W, PAGE)), const((1, GROUP_W)), const((GROUP_W, GROUP_W))]
    args = [lam.reshape(1), wq, add, k_new, v_new, gain256.reshape(1, GROUP_W), _hgrn_consts(128)[2]]
    if use_sel:
        in_specs += [pl.BlockSpec((None, pps, DEC_Q, PAGE), lambda r, g, pt: (r, g, 0, 0)),
                     pl.BlockSpec((None, 1, DEC_Q, PAGE), lambda r, g, pt: (r, n_pages, 0, 0))]
        args += [sel, sel]
    in_specs += [page_spec(i) for i in range(pps)] * 2
    args += [ck] * pps + [cv] * pps
    return pl.pallas_call(
        functools.partial(_paged_body, n_pages=n_pages, pps=pps, rows=rows, diff=diff, use_sel=use_sel,
                          out_scale=out_scale),
        grid_spec=pltpu.PrefetchScalarGridSpec(
            num_scalar_prefetch=1,
            grid=(nreq, n_pages // pps),
            in_specs=in_specs,
            out_specs=pl.BlockSpec((DEC_Q, GROUP_W), lambda r, g, pt: (r, 0)),
            scratch_shapes=[pltpu.VMEM((rows, PAGE), F32), pltpu.VMEM((rows, PAGE), F32),
                            pltpu.VMEM((rows, GROUP_W), F32)]),
        out_shape=jax.ShapeDtypeStruct((nreq * DEC_Q, GROUP_W), F32),
        compiler_params=_cparams(("parallel", "arbitrary")),
        name="paged_diff" if diff else "paged_dsa",
    )(page_table.reshape(-1), *args)


def _decode_add_tiles(tab, past, groups_per_head):
    t = np.arange(DEC_Q)[:, None]
    c = np.arange(PAGE)[None, :]
    far = jnp.asarray(np.broadcast_to(past + t - (past - PAGE - 1), (DEC_Q, PAGE)), I32)
    last = jnp.asarray(t + PAGE - c, I32)
    new = jnp.asarray(t - c, I32)
    ok_new = jnp.asarray((c <= t) & (c < DEC_Q))
    tiles = []
    for rel, ok in ((far, None), (last, None), (new, ok_new)):
        b = jnp.transpose(_bucket_lookup(tab, rel), (2, 0, 1))
        if ok is not None:
            b = jnp.where(ok[None], b, NEG)
        tiles.append(jnp.repeat(b, groups_per_head, axis=0).reshape(-1, PAGE))
    return jnp.stack(tiles, axis=0)


def _masked_queries(q, n_groups, width, scale):
    lane = np.arange(GROUP_W)[None, :] // width == np.arange(n_groups)[:, None]
    m = jnp.asarray(lane, F32)[None, :, None, :] * (q * scale)[:, None, :, :]
    return m.reshape(q.shape[0], n_groups * DEC_Q, GROUP_W).astype(BF16)


def _pad_new(a):
    return jnp.pad(jnp.swapaxes(a, 1, 2), ((0, 0), (0, 0), (0, PAGE - DEC_Q))).astype(BF16)


def _feature_major_pages(cache):
    d, n_pool = cache.shape[:2]
    return jnp.swapaxes(cache.reshape(d, n_pool, PAGE, -1), 2, 3)


SEL_CHUNK = 32


def _dsa_sel_body(pt_ref, wqi_ref, wcol_ref, kinew_ref, tri_ref, *rest, n_pages, pps, n_pad, n_sel):
    del pt_ref
    ki_refs = rest[:pps]
    sel_ref, key_ref = rest[pps:]
    g = pl.program_id(1)
    wqi = wqi_ref[...]
    wcol = wcol_ref[...]

    def score(kts):
        kt = kts[0] if len(kts) == 1 else jnp.concatenate(kts, axis=1)
        s = jnp.maximum(_dot(wqi, kt), 0.0) * jnp.concatenate([wcol] * len(kts), axis=1)
        out = s[0:DEC_Q]
        for e in range(1, IDX_HEADS):
            out = out + s[e * DEC_Q:(e + 1) * DEC_Q]
        return out

    keys = _sort_key(score([ki_refs[i][...].astype(BF16) for i in range(pps)]))
    for i in range(pps):
        rows = pl.ds(pl.multiple_of((g * pps + i) * DEC_Q, DEC_Q), DEC_Q)
        key_ref[rows, :] = keys[:, i * PAGE:(i + 1) * PAGE]

    @pl.when(g == n_pages // pps - 1)
    def _():
        t = lax.broadcasted_iota(I32, (DEC_Q, PAGE), 0)
        c = lax.broadcasted_iota(I32, (DEC_Q, PAGE), 1)
        key_ref[n_pages * DEC_Q:(n_pages + 1) * DEC_Q, :] = jnp.where(c <= t, _sort_key(score([kinew_ref[...]])), INT_MIN)
        if n_pad > n_pages + 1:
            key_ref[(n_pages + 1) * DEC_Q:, :] = jnp.full(((n_pad - n_pages - 1) * DEC_Q, PAGE), INT_MIN, I32)

        def count(pred):
            tot = jnp.zeros((COUNT_ROWS, PAGE), F32)
            for ch in range(n_pad // SEL_CHUNK):
                hit = jnp.where(pred(key_ref[ch * SEL_CHUNK * DEC_Q:(ch + 1) * SEL_CHUNK * DEC_Q, :]), 1.0, 0.0)
                tot = tot + jnp.sum(hit.reshape(SEL_CHUNK * DEC_Q // COUNT_ROWS, COUNT_ROWS, PAGE), axis=0)
            tot = jnp.sum(tot.reshape(COUNT_ROWS // DEC_Q, DEC_Q, PAGE), axis=0)
            return jnp.sum(tot, axis=-1, keepdims=True)

        tile_rows = lambda x: jnp.tile(jnp.broadcast_to(x, (DEC_Q, PAGE)), (SEL_CHUNK, 1))
        thr = _kth_largest(lambda cand: count(lambda k, cb=tile_rows(cand): k >= cb), float(n_sel), (DEC_Q, 1))
        n_ge = count(lambda k, tb=tile_rows(thr): k >= tb)
        no_surplus = jnp.max(n_ge) <= float(n_sel)

        @pl.when(no_surplus)
        def _():
            thr_b = jnp.broadcast_to(thr, (DEC_Q, PAGE))
            for j in range(n_pages + 1):
                sel_ref[j] = jnp.where(key_ref[j * DEC_Q:(j + 1) * DEC_Q, :] >= thr_b, 0.0, NEG)

        @pl.when(jnp.logical_not(no_surplus))
        def _():
            room = float(n_sel) - count(lambda k, tb=tile_rows(thr): k > tb)

            def select(j, seen):
                k = key_ref[pl.ds(pl.multiple_of(j * DEC_Q, DEC_Q), DEC_Q), :]
                eq = jnp.where(k == thr, 1.0, 0.0)
                rank = seen + _dot(eq.astype(BF16), tri_ref[...])
                keep = (k > thr) | ((k == thr) & (rank <= room))
                sel_ref[j] = jnp.where(keep, 0.0, NEG)
                return seen + jnp.sum(eq, axis=-1, keepdims=True)

            lax.fori_loop(0, n_pages + 1, select, jnp.zeros((DEC_Q, 1), F32))


def _dsa_decode_select(page_table, wqi, wcol, ki_new, cache_ki, l):
    nreq, n_pages = page_table.shape
    n_sel = min(TOPK_MAX, (n_pages * PAGE + DEC_Q) // 4)
    n_pad = -(-(n_pages + 1) // SEL_CHUNK) * SEL_CHUNK
    tri = jnp.asarray(np.arange(PAGE)[:, None] <= np.arange(PAGE)[None, :], BF16)
    rows = IDX_HEADS * DEC_Q
    pps = min(PPS_SELECT, n_pages)
    page_spec = lambda i: pl.BlockSpec((None, None, IDX_DIM, PAGE),
                                       lambda r, g, pt, i=i: (l, pt[r * n_pages + g * pps + i], 0, 0))
    cache_ki = _feature_major_pages(cache_ki)
    return pl.pallas_call(
        functools.partial(_dsa_sel_body, n_pages=n_pages, pps=pps, n_pad=n_pad, n_sel=n_sel),
        grid_spec=pltpu.PrefetchScalarGridSpec(
            num_scalar_prefetch=1,
            grid=(nreq, n_pages // pps),
            in_specs=[pl.BlockSpec((None, rows, IDX_DIM), lambda r, g, pt: (r, 0, 0)),
                      pl.BlockSpec((None, rows, PAGE), lambda r, g, pt: (r, 0, 0)),
                      pl.BlockSpec((None, IDX_DIM, PAGE), lambda r, g, pt: (r, 0, 0)),
                      pl.BlockSpec((PAGE, PAGE), lambda r, g, pt: (0, 0))]
            + [page_spec(i) for i in range(pps)],
            out_specs=pl.BlockSpec((None, n_pages + 1, DEC_Q, PAGE), lambda r, g, pt: (r, 0, 0, 0)),
            scratch_shapes=[pltpu.VMEM((n_pad * DEC_Q, PAGE), I32)]),
        out_shape=jax.ShapeDtypeStruct((nreq, n_pages + 1, DEC_Q, PAGE), F32),
        compiler_params=_cparams(("parallel", "arbitrary")),
        name="dsa_select",
    )(page_table.reshape(-1), wqi, wcol, ki_new, tri, *([cache_ki] * pps))


def _layer_params(l, prm):
    f32 = F32
    lb_all = jnp.cumsum(jax.nn.softmax(prm['hgrn_lb_logits'].astype(f32), axis=0), axis=0)
    lb = lb_all[l] - lb_all[0]
    lam_init = 0.8 - 0.6 * math.exp(-0.3 * l)
    lam = (jnp.exp(jnp.sum(prm['lq1'][l] * prm['lk1'][l])) - jnp.exp(jnp.sum(prm['lq2'][l] * prm['lk2'][l])) + lam_init)
    w_in = jnp.pad(prm['w_in'][l], ((0, 0), (0, D_IN_PAD - D_IN))).astype(BF16)
    return dict(
        w_in=w_in, log_lb=jnp.maximum(jnp.log(lb), NEG), log1m_lb=jnp.log1p(-lb), lam=lam.astype(f32), lam_init=lam_init,
        bias_b=_bias_tiles(prm['rel_bias'][:, :N_HEADS]), bias_d=_bias_tiles(prm['rel_bias'][:, N_HEADS:]),
        w_out=prm['w_out'][l].astype(BF16), w_gu=prm['w_gate_up'][l].astype(BF16), w_down=prm['w_down'][l].astype(BF16))


def _layer_prompt(x2d, bsz, seq, l, prm, lp, final):
    n = bsz * seq
    p = _proj(x2d, prm['attn_norm'][l], lp['w_in'])
    cols = lambda name: p[:, GROUP_W * COL[name]:GROUP_W * (COL[name] + 1)]
    b_k, b_v, d_k, d_v = cols('b_k'), cols('b_v'), cols('d_k'), cols('d_v')
    d_ki = p[:, 128 * TAIL_COL128:128 * TAIL_COL128 + IDX_DIM]

    st0 = jnp.zeros((bsz, GROUP_W, GROUP_W), F32)
    y_a, st = _hgrn(p, bsz, seq, lp['log_lb'], lp['log1m_lb'], prm['hgrn_norm'][l], st0, seq)

    y_b = _diff_prompt(p, bsz, seq, b_k.astype(BF16).reshape(bsz, seq, GROUP_W), _value_blocks_t(b_v, bsz, seq, TK),
                       lp['lam'], lp['bias_b'], prm['diff_norm'][l], 1.0 - lp['lam_init'])

    mix, bias_tile = _sgu_params(prm['sgu_w'][l], prm['sgu_b'][l], seq)
    y_c, c_v = _sgu(p, prm['sgu_norm'][l], mix, bias_tile)

    ki_rep = jnp.tile(d_ki, (1, IDX_HEADS)).astype(BF16).reshape(bsz, seq, GROUP_W)
    w_t = jnp.transpose(p[:, 128 * TAIL_COL128 + IDX_DIM:128 * TAIL_COL128 + IDX_DIM + IDX_HEADS])
    y_d = _dsa_prompt(p, bsz, seq, d_k.astype(BF16).reshape(bsz, seq, GROUP_W), _value_blocks_t(d_v, bsz, seq, TK),
                      ki_rep, w_t, lp['bias_d'])

    x_new = _post(x2d, (y_a, y_b, y_c, y_d), prm['mix_scale'][l], lp['w_out'], prm['ffn_norm'][l], lp['w_gu'],
                  lp['w_down'], prm['final_norm'], final)
    hd = lambda a: a.reshape(bsz, seq, N_HEADS, HEAD_DV)
    return x_new, (hd(b_k), hd(b_v), hd(d_k), hd(d_v), d_ki.reshape(bsz, seq, IDX_DIM), _blockdiag_to_state(st),
                   c_v.reshape(bsz, seq, GROUP_W))


def _layer_decode(x2d, nreq, l, prm, lp, past, final):
    n = nreq * DEC_Q
    pt = past['page_table']
    past_len = pt.shape[1] * PAGE
    p = _proj(x2d, prm['attn_norm'][l], lp['w_in'])
    cols = lambda name: p[:, GROUP_W * COL[name]:GROUP_W * (COL[name] + 1)]
    per_req = lambda a: a.reshape(nreq, DEC_Q, a.shape[-1])
    b_k, b_v, d_k, d_v = cols('b_k'), cols('b_v'), cols('d_k'), cols('d_v')
    tail = p[:, 128 * TAIL_COL128:128 * (TAIL_COL128 + 1)]
    d_ki = tail[:, :IDX_DIM]

    p16 = jnp.pad(per_req(p), ((0, 0), (0, HG - DEC_Q), (0, 0))).reshape(nreq * HG, D_IN_PAD)
    y_a16, st = _hgrn(p16, nreq, HG, lp['log_lb'], lp['log1m_lb'], prm['hgrn_norm'][l],
                      _state_to_blockdiag(past['hgrn'][l]), DEC_Q)
    y_a = y_a16.reshape(nreq, HG, GROUP_W)[:, :DEC_Q].reshape(n, GROUP_W)

    mix, bias_tile = _sgu_params(prm['sgu_w'][l], prm['sgu_b'][l], DEC_Q)
    y_c, c_v = _sgu(p, prm['sgu_norm'][l], mix, bias_tile)

    y_b = _paged_attn(pt, lp['lam'], _masked_queries(per_req(cols('b_q')), 2 * N_HEADS, DIFF_DQ, DIFF_DQ ** -0.5),
                      _decode_add_tiles(prm['rel_bias'][:, :N_HEADS], past_len, 2), _pad_new(per_req(b_k)),
                      _pad_new(per_req(b_v)), jnp.tile(prm['diff_norm'][l], N_HEADS), None,
                      past['k_diff'], past['v_diff'], l, diff=True, out_scale=1.0 - lp['lam_init'])

    qi = per_req(cols('d_qi')).reshape(nreq, DEC_Q, IDX_HEADS, IDX_DIM)
    wqi = jnp.swapaxes(qi, 1, 2).reshape(nreq, IDX_HEADS * DEC_Q, IDX_DIM).astype(BF16)
    w = per_req(tail[:, IDX_DIM:IDX_DIM + IDX_HEADS] * (IDX_HEADS ** -0.5 * IDX_DIM ** -0.5))
    wcol = jnp.broadcast_to(jnp.swapaxes(w, 1, 2).reshape(nreq, IDX_HEADS * DEC_Q, 1), (nreq, IDX_HEADS * DEC_Q, PAGE))
    sel = _dsa_decode_select(pt, wqi, wcol, _pad_new(per_req(d_ki)), past['k_index'], l)
    y_d = _paged_attn(pt, lp['lam'], _masked_queries(per_req(cols('d_q')), N_HEADS, HEAD_DV, HEAD_DV ** -0.5),
                      _decode_add_tiles(prm['rel_bias'][:, N_HEADS:], past_len, 1), _pad_new(per_req(d_k)),
                      _pad_new(per_req(d_v)), jnp.ones((GROUP_W,), F32), sel,
                      past['k_sparse'], past['v_sparse'], l, diff=False, out_scale=1.0)

    x_new = _post(x2d, (y_a, y_b, y_c, y_d), prm['mix_scale'][l], lp['w_out'], prm['ffn_norm'][l], lp['w_gu'],
                  lp['w_down'], prm['final_norm'], final)
    hd = lambda a: a.reshape(nreq, DEC_Q, N_HEADS, HEAD_DV)
    return x_new, (hd(b_k), hd(b_v), hd(d_k), hd(d_v), d_ki.reshape(nreq, DEC_Q, IDX_DIM), _blockdiag_to_state(st),
                   c_v.reshape(nreq, DEC_Q, GROUP_W))


def kernel(x_prompt, x_sample, cache_k_diff, cache_v_diff, cache_k_sparse, cache_v_sparse, cache_k_index, state_hgrn, page_table, w_in, w_out, attn_norm, ffn_norm, final_norm, w_gate_up, w_down, hgrn_lb_logits, hgrn_norm, diff_lambda_q1, diff_lambda_k1, diff_lambda_q2, diff_lambda_k2, diff_norm, sgu_w, sgu_b, sgu_norm, mix_scale, rel_bias):
    prm = dict(w_in=w_in, w_out=w_out, attn_norm=attn_norm, ffn_norm=ffn_norm, final_norm=final_norm,
               w_gate_up=w_gate_up, w_down=w_down, hgrn_lb_logits=hgrn_lb_logits, hgrn_norm=hgrn_norm,
               lq1=diff_lambda_q1, lk1=diff_lambda_k1, lq2=diff_lambda_q2, lk2=diff_lambda_k2, diff_norm=diff_norm,
               sgu_w=sgu_w, sgu_b=sgu_b, sgu_norm=sgu_norm, mix_scale=mix_scale, rel_bias=rel_bias)
    past = dict(page_table=page_table, k_diff=cache_k_diff, v_diff=cache_v_diff, k_sparse=cache_k_sparse,
                v_sparse=cache_v_sparse, k_index=cache_k_index, hgrn=state_hgrn)
    depth = w_in.shape[0]
    bsz, seq, d = x_prompt.shape
    nreq = x_sample.shape[0]
    xp = x_prompt.reshape(bsz * seq, d)
    xs = x_sample.reshape(nreq * DEC_Q, d)
    st_p, st_s = [], []
    for l in range(depth):
        lp = _layer_params(l, prm)
        final = l == depth - 1
        xp, sp = _layer_prompt(xp, bsz, seq, l, prm, lp, final)
        xs, ss = _layer_decode(xs, nreq, l, prm, lp, past, final)
        st_p.append(sp)
        st_s.append(ss)
    stk = lambda lst, i: jnp.stack([s[i] for s in lst], axis=0)
    return (xp.reshape(bsz, seq, d), xs.reshape(nreq, DEC_Q, d),
            stk(st_p, 0), stk(st_p, 1), stk(st_p, 2), stk(st_p, 3), stk(st_p, 4), stk(st_p, 5),
            stk(st_s, 0), stk(st_s, 1), stk(st_s, 2), stk(st_s, 3), stk(st_s, 4), stk(st_s, 5), stk(st_s, 6))
```

```python
import functools
import math

import numpy as np
import jax
import jax.numpy as jnp
from jax import lax
from jax.experimental import pallas as pl
from jax.experimental.pallas import tpu as pltpu

F32 = jnp.float32
BF16 = jnp.bfloat16
I32 = jnp.int32

D_MODEL = 1024
GROUP_W = 256
N_HEADS = 4
HEAD_DV = 64
DIFF_DQ = 32
IDX_HEADS = 8
IDX_DIM = 32
TOPK_MAX = 256
REL_BUCKETS = 32
REL_EXACT = 16
REL_MAX_DIST = 128
PAGE = 128
D_FF = 2816
EPS = 1e-6
D_IN = 3368
D_IN_PAD = 3456
COL = {'a_q': 0, 'a_f': 1, 'a_i': 2, 'a_g': 3, 'b_q': 4, 'b_k': 5, 'b_v': 6, 'c_u': 7, 'c_v': 8,
       'd_q': 9, 'd_k': 10, 'd_v': 11, 'd_qi': 12}
TAIL_COL128 = 26
NEG = -1e30
INT_MIN = -2 ** 31
TQ = 128
HG = 16
HGRN_PAIR = 4
HGRN_ROWS = 512
ROW_TILE = 512
VMEM_LIMIT = 56 * 1024 * 1024


def _cparams(sem):
    return pltpu.CompilerParams(dimension_semantics=sem, vmem_limit_bytes=VMEM_LIMIT)


def _dot(a, b):
    return jnp.dot(a, b, preferred_element_type=F32)


def _dot_nt(a, b):
    return lax.dot_general(a, b, (((1,), (1,)), ((), ())), preferred_element_type=F32)


def _dot_tn(a, b):
    return lax.dot_general(a, b, (((0,), (0,)), ((), ())), preferred_element_type=F32)


def _sigmoid(x):
    return 1.0 / (1.0 + jnp.exp(-x))


def _split3(x):
    hi = x.astype(BF16)
    r = x - hi.astype(F32)
    mid = r.astype(BF16)
    lo = (r - mid.astype(F32)).astype(BF16)
    return hi, mid, lo


def _dot_exact_lhs(m_bf16, x):
    hi, mid, lo = _split3(x)
    return _dot(m_bf16, hi) + _dot(m_bf16, mid) + _dot(m_bf16, lo)


def _proj_body(x_ref, g_ref, w_ref, o_ref):
    x = x_ref[...]
    h = x * lax.rsqrt(jnp.mean(x * x, axis=-1, keepdims=True) + EPS) * g_ref[...]
    o_ref[...] = _dot(h.astype(BF16), w_ref[...])


def _proj(x2d, gain, w_bf16):
    n, d = x2d.shape
    dn = w_bf16.shape[1]
    tm = min(ROW_TILE, n)
    return pl.pallas_call(
        _proj_body,
        grid=(n // tm,),
        in_specs=[pl.BlockSpec((tm, d), lambda i: (i, 0)),
                  pl.BlockSpec((1, d), lambda i: (0, 0)),
                  pl.BlockSpec((d, dn), lambda i: (0, 0))],
        out_specs=pl.BlockSpec((tm, dn), lambda i: (i, 0)),
        out_shape=jax.ShapeDtypeStruct((n, dn), F32),
        compiler_params=_cparams(("parallel",)),
        name="proj",
    )(x2d, gain.reshape(1, d), w_bf16)


FF_CHUNK = 256


def _outproj_body(x_ref, ya_ref, yb_ref, yc_ref, yd_ref, ms_ref, wo_ref, o_ref):
    acc = x_ref[...]
    for i, y_ref in enumerate((ya_ref, yb_ref, yc_ref, yd_ref)):
        lo, hi = GROUP_W * i, GROUP_W * (i + 1)
        yi = (y_ref[...] * ms_ref[:, lo:hi]).astype(BF16)
        acc = acc + _dot(yi, wo_ref[lo:hi, :])
    o_ref[...] = acc


def _ffn_body(x_ref, fn_ref, wgu_ref, wd_ref, fin_ref, o_ref, acc_ref, *, final):
    x = x_ref[...]
    acc_ref[...] = x
    h = (x * lax.rsqrt(jnp.mean(x * x, axis=-1, keepdims=True) + EPS) * fn_ref[...]).astype(BF16)
    for c in range(D_FF // FF_CHUNK):
        lo, hi = FF_CHUNK * c, FF_CHUNK * (c + 1)
        g = _dot(h, wgu_ref[:, lo:hi])
        u = _dot(h, wgu_ref[:, D_FF + lo:D_FF + hi])
        a = (g * _sigmoid(g) * u).astype(BF16)
        acc_ref[...] += _dot(a, wd_ref[lo:hi, :])
    out = acc_ref[...]
    if final:
        out = out * lax.rsqrt(jnp.mean(out * out, axis=-1, keepdims=True) + EPS) * fin_ref[...]
    o_ref[...] = out


def _post(x2d, ys, mix_scale, wo, ffn_norm, wgu, wd, final_norm, final):
    n, d = x2d.shape
    tm = min(ROW_TILE, n)
    row = lambda i: (i, 0)
    const = lambda i: (0, 0)
    x1 = pl.pallas_call(
        _outproj_body,
        grid=(n // tm,),
        in_specs=[pl.BlockSpec((tm, d), row)]
        + [pl.BlockSpec((tm, GROUP_W), row) for _ in range(4)]
        + [pl.BlockSpec((1, d), const), pl.BlockSpec((d, d), const)],
        out_specs=pl.BlockSpec((tm, d), row),
        out_shape=jax.ShapeDtypeStruct((n, d), F32),
        compiler_params=_cparams(("parallel",)),
        name="outproj",
    )(x2d, *ys, mix_scale.reshape(1, d), wo)
    return pl.pallas_call(
        functools.partial(_ffn_body, final=final),
        grid=(n // tm,),
        in_specs=[pl.BlockSpec((tm, d), row),
                  pl.BlockSpec((1, d), const),
                  pl.BlockSpec((d, 2 * D_FF), const),
                  pl.BlockSpec((D_FF, d), const),
                  pl.BlockSpec((1, d), const)],
        out_specs=pl.BlockSpec((tm, d), row),
        out_shape=jax.ShapeDtypeStruct((n, d), F32),
        scratch_shapes=[pltpu.VMEM((tm, d), F32)],
        compiler_params=_cparams(("parallel",)),
        name="ffn",
    )(x1, ffn_norm.reshape(1, d), wgu, wd, final_norm.reshape(1, d))


def _hgrn_consts(tr):
    r = np.arange(tr)
    same = (r[:, None] // HG) == (r[None, :] // HG)
    tri = same & (r[None, :] <= r[:, None])
    c = np.arange(GROUP_W)
    heads = (c[:, None] // HEAD_DV) == (c[None, :] // HEAD_DV)
    return (jnp.asarray(tri, BF16), jnp.asarray(same, BF16), jnp.asarray(heads, BF16), jnp.asarray(heads, F32))


def _shift_rows(x, d, row):
    if d == 0:
        return x
    return jnp.where(row >= d, pltpu.roll(x, d, axis=0), 0.0)


def _hgrn_body(aq_ref, af_ref, ai_ref, ag_ref, la_ref, l1_ref, hn_ref, tri_ref, same_ref, e_ref, mask_ref, st0_ref,
               ya_ref, st_ref, cum_ref, tot_ref, kk_ref, qq_ref, qe_ref, ke_ref, o_ref, *, sc, tr, valid_rows, seq):
    c = pl.program_id(1)

    @pl.when(c == 0)
    def _():
        st_ref[...] = st0_ref[...]

    def prep(t, carry):
        r0 = pl.multiple_of(t * tr, tr)
        rows = pl.ds(r0, tr)
        for e in range(HGRN_PAIR):
            z = af_ref[e, rows, :]
            ls = jnp.minimum(z, 0.0) - jnp.log(1.0 + jnp.exp(-jnp.abs(z)))
            b = l1_ref[...] + ls
            a = la_ref[...]
            lf = jnp.maximum(a, b) + jnp.log(1.0 + jnp.exp(-jnp.abs(a - b)))
            kk = 1.0 - jnp.exp(lf)
            if valid_rows < seq:
                ok = (lax.broadcasted_iota(I32, (tr, GROUP_W), 0) + r0 + c * sc) < valid_rows
                lf = jnp.where(ok, lf, 0.0)
                kk = jnp.where(ok, kk, 0.0)
            cum = _dot_exact_lhs(tri_ref[...], lf)
            tot = _dot_exact_lhs(same_ref[...], lf)
            q = aq_ref[e, rows, :]
            qq = q * _sigmoid(q)
            cum_ref[e, rows, :] = cum
            tot_ref[e, rows, :] = tot
            kk_ref[e, rows, :] = kk
            qq_ref[e, rows, :] = qq
            qe_ref[e, rows, :] = (qq * jnp.exp(cum)).astype(BF16)
            ke_ref[e, rows, :] = (kk * jnp.exp(tot - cum)).astype(BF16)
        return carry

    lax.fori_loop(0, sc // tr, prep, 0)

    row = lax.broadcasted_iota(I32, (HG, GROUP_W), 0)

    def block(i, carry):
        rows = pl.ds(pl.multiple_of(i * HG, HG), HG)
        for e in range(HGRN_PAIR):
            cum = cum_ref[e, rows, :]
            kk = kk_ref[e, rows, :]
            qq = qq_ref[e, rows, :]
            v = ai_ref[e, rows, :]
            o = _dot_nt(qe_ref[e, rows, :], st_ref[e].astype(BF16))
            for d in range(HG):
                x = qq * _shift_rows(kk, d, row) * jnp.exp(cum - _shift_rows(cum, d, row))
                o = o + _dot(x.astype(BF16), e_ref[...]) * _shift_rows(v, d, row)
            o_ref[e, rows, :] = o
            dec = jnp.exp(tot_ref[e, pl.ds(pl.multiple_of(i * HG, HG), 1), :])
            upd = _dot_tn(v.astype(BF16), ke_ref[e, rows, :])
            st_ref[e] = st_ref[e] * dec + mask_ref[...] * upd
        return carry

    lax.fori_loop(0, sc // HG, block, 0)

    def fin(t, carry):
        rows = pl.ds(pl.multiple_of(t * tr, tr), tr)
        for e in range(HGRN_PAIR):
            o = o_ref[e, rows, :]
            hi = (o * o).astype(BF16)
            lo = (o * o - hi.astype(F32)).astype(BF16)
            msq = (_dot(hi, e_ref[...]) + _dot(lo, e_ref[...])) * (1.0 / HEAD_DV)
            g = ag_ref[e, rows, :]
            ya_ref[e, rows, :] = o * lax.rsqrt(msq + EPS) * hn_ref[...] * (g * _sigmoid(g))
        return carry

    lax.fori_loop(0, sc // tr, fin, 0)


def _hgrn(p2d, bsz, seq, log_lb, log1m_lb, hnorm, st0, valid_rows):
    assert bsz % HGRN_PAIR == 0
    sc = min(HGRN_ROWS, seq)
    tr = min(128, sc)
    tri, same, e_bf, mask = _hgrn_consts(tr)
    col = lambda name: pl.BlockSpec((HGRN_PAIR, sc, GROUP_W), lambda b, c, k=COL[name]: (b, c, k))
    vec = pl.BlockSpec((1, GROUP_W), lambda b, c: (0, 0))
    mat = lambda n: pl.BlockSpec((n, n), lambda b, c: (0, 0))
    st_spec = pl.BlockSpec((HGRN_PAIR, GROUP_W, GROUP_W), lambda b, c: (b, 0, 0))
    big = lambda dt: pltpu.VMEM((HGRN_PAIR, sc, GROUP_W), dt)
    p3d = p2d.reshape(bsz, seq, p2d.shape[1])
    ya, st = pl.pallas_call(
        functools.partial(_hgrn_body, sc=sc, tr=tr, valid_rows=valid_rows, seq=seq),
        grid=(bsz // HGRN_PAIR, seq // sc),
        in_specs=[col('a_q'), col('a_f'), col('a_i'), col('a_g'), vec, vec, vec, mat(tr), mat(tr),
                  mat(GROUP_W), mat(GROUP_W), st_spec],
        out_specs=[pl.BlockSpec((HGRN_PAIR, sc, GROUP_W), lambda b, c: (b, c, 0)), st_spec],
        out_shape=[jax.ShapeDtypeStruct((bsz, seq, GROUP_W), F32),
                   jax.ShapeDtypeStruct((bsz, GROUP_W, GROUP_W), F32)],
        scratch_shapes=[big(F32), big(F32), big(F32), big(F32), big(BF16), big(BF16), big(F32)],
        compiler_params=_cparams(("parallel", "arbitrary")),
        name="hgrn",
    )(p3d, p3d, p3d, p3d, log_lb.reshape(1, GROUP_W), log1m_lb.reshape(1, GROUP_W), hnorm.reshape(1, GROUP_W),
      tri, same, e_bf, mask, st0)
    return ya.reshape(bsz * seq, GROUP_W), st


def _state_to_blockdiag(s):
    st = jnp.swapaxes(s, 2, 3)
    eye = jnp.eye(N_HEADS, dtype=s.dtype)
    return jnp.einsum('bhvk,hg->bhvgk', st, eye).reshape(s.shape[0], GROUP_W, GROUP_W)


def _blockdiag_to_state(st):
    b = st.shape[0]
    st5 = st.reshape(b, N_HEADS, HEAD_DV, N_HEADS, HEAD_DV)
    diag = jnp.stack([st5[:, h, :, h, :] for h in range(N_HEADS)], axis=1)
    return jnp.swapaxes(diag, 2, 3)


def _gelu(x):
    return 0.5 * x * (1.0 + lax.erf(x * (2.0 ** -0.5)))


def _sgu_body(cu_ref, cv_ref, sn_ref, mix_ref, bias_ref, yc_ref, cvn_ref, *, tr):
    u = _gelu(cu_ref[...])
    v = _gelu(cv_ref[...])
    vn = v * lax.rsqrt(jnp.mean(v * v, axis=-1, keepdims=True) + EPS) * sn_ref[...]
    cvn_ref[...] = vn
    vb = vn.astype(BF16)
    lane = lax.broadcasted_iota(I32, (tr, GROUP_W), 1)
    mixed = bias_ref[...]
    for g in range(N_HEADS):
        mg = _dot(mix_ref[g], vb)
        mixed = mixed + jnp.where((lane >= HEAD_DV * g) & (lane < HEAD_DV * (g + 1)), mg, 0.0)
    yc_ref[...] = u * mixed


def _sgu(p2d, snorm, mix_bf16, bias_tile):
    n = p2d.shape[0]
    tr = 128
    col = lambda name: pl.BlockSpec((tr, GROUP_W), lambda i, c=COL[name]: (i, c))
    out = pl.BlockSpec((tr, GROUP_W), lambda i: (i, 0))
    return pl.pallas_call(
        functools.partial(_sgu_body, tr=tr),
        grid=(n // tr,),
        in_specs=[col('c_u'), col('c_v'),
                  pl.BlockSpec((1, GROUP_W), lambda i: (0, 0)),
                  pl.BlockSpec((N_HEADS, tr, tr), lambda i: (0, 0, 0)),
                  pl.BlockSpec((tr, GROUP_W), lambda i: (0, 0))],
        out_specs=[out, out],
        out_shape=[jax.ShapeDtypeStruct((n, GROUP_W), F32)] * 2,
        compiler_params=_cparams(("parallel",)),
        name="sgu",
    )(p2d, p2d, snorm.reshape(1, GROUP_W), mix_bf16, bias_tile)


def _sgu_params(w, b, seq):
    c = min(128, seq)
    wt = jnp.tril(w[:, :c, :c])
    mix = jnp.einsum('rs,gtu->grtsu', jnp.eye(128 // c, dtype=w.dtype), wt).reshape(N_HEADS, 128, 128)
    bt = jnp.tile(jnp.transpose(b[:, :c]), (128 // c, 1))
    return mix.astype(BF16), jnp.repeat(bt, HEAD_DV, axis=1)


def _rel_bucket(rel):
    n = jnp.maximum(rel, 0)
    nf = jnp.maximum(n, 1).astype(F32)
    large = REL_EXACT + (jnp.log(nf / REL_EXACT) / math.log(REL_MAX_DIST / REL_EXACT)
                         * (REL_BUCKETS - REL_EXACT)).astype(I32)
    large = jnp.minimum(large, REL_BUCKETS - 1)
    return jnp.where(n < REL_EXACT, n, large)


def _bucket_lookup(tab, rel):
    onehot = (_rel_bucket(rel)[..., None] == jnp.arange(REL_BUCKETS, dtype=I32)).astype(F32)
    return jnp.einsum('...b,bh->...h', onehot, tab.astype(F32), precision=lax.Precision.HIGHEST)


def _bias_tiles(tab):
    assert REL_MAX_DIST <= TQ
    q = np.arange(TQ)[:, None]
    k = np.arange(TQ)[None, :]
    tiles = []
    for back in range(3):
        rel = jnp.asarray(q - k + TQ * back, I32)
        t = jnp.transpose(_bucket_lookup(tab, rel), (2, 0, 1))
        if back == 0:
            t = jnp.where(jnp.asarray(k <= q)[None], t, NEG)
        tiles.append(t)
    return jnp.stack(tiles, axis=0)


def _lane_band(x, lo, width):
    lane = lax.broadcasted_iota(I32, x.shape, x.ndim - 1)
    return jnp.where((lane >= lo) & (lane < lo + width), x, 0.0)


TK = 2 * TQ
COUNT_ROWS = 32


def _value_blocks_t(v2d, bsz, seq, tk):
    v = jnp.transpose(v2d.reshape(bsz, seq // tk, tk, N_HEADS, HEAD_DV), (0, 1, 3, 4, 2))
    return jnp.concatenate([v, jnp.ones_like(v)], axis=3).astype(BF16)


def _bias_tiles_t(bias, maps):
    t = jnp.transpose(bias, (0, 3, 1, 2))[:, :, :, None, :]
    t = jnp.broadcast_to(t, (3, TQ, N_HEADS, maps, TQ)).reshape(3, TQ, N_HEADS * maps * TQ)
    return jnp.concatenate([t, jnp.full_like(t[:1], NEG)], axis=0)


def _step_bias(bias_ref, back):
    first = bias_ref[jnp.where(back < 0, 3, jnp.minimum(back, 2))]
    second = bias_ref[jnp.where(back < 1, 3, jnp.minimum(back - 1, 2))]
    return jnp.concatenate([first, second], axis=0)


def _diff_body(lam_ref, q_ref, k_ref, vt_ref, bias_ref, gain_ref, o_ref, qm_ref, m_ref, acc_ref, *, out_scale):
    i = pl.program_id(1)
    q = q_ref[...] * (DIFF_DQ ** -0.5)
    for hc in range(2 * N_HEADS):
        qm_ref[hc * TQ:(hc + 1) * TQ, :] = _lane_band(q, DIFF_DQ * hc, DIFF_DQ).astype(BF16)
    m_ref[...] = jnp.full(m_ref.shape, NEG, F32)
    acc_ref[...] = jnp.zeros(acc_ref.shape, F32)

    n_step = (i + 2) // 2

    def step_pair(tt, carry):
        for u in range(2):
            t = 2 * tt + u
            tc = jnp.minimum(t, n_step - 1)
            s = (_dot_nt(k_ref[pl.ds(pl.multiple_of(tc * TK, TK), TK), :], qm_ref[...])
                 + _step_bias(bias_ref, jnp.where(t < n_step, i - 2 * t, -2)))
            m_old = m_ref[...]
            m_new = jnp.maximum(m_old, jnp.max(s, axis=0, keepdims=True))
            alpha = jnp.exp(m_old - m_new)
            p = jnp.exp(s - m_new).astype(BF16)
            for h in range(N_HEADS):
                cols = slice(2 * h * TQ, (2 * h + 2) * TQ)
                acc_ref[h] = alpha[:, cols] * acc_ref[h] + _dot(vt_ref[tc, h], p[:, cols])
            m_ref[...] = m_new
        return carry

    lax.fori_loop(0, (n_step + 1) // 2, step_pair, 0)

    lam = lam_ref[0]
    res = []
    for h in range(N_HEADS):
        a = acc_ref[h]
        n = a[:HEAD_DV, :] / a[HEAD_DV:HEAD_DV + 1, :]
        o = n[:, :TQ] - lam * n[:, TQ:]
        msq = jnp.mean(o * o, axis=0, keepdims=True)
        res.append(o * lax.rsqrt(msq + EPS) * gain_ref[...] * out_scale)
    o_ref[...] = jnp.transpose(jnp.concatenate(res, axis=0))


def _diff_prompt(p2d, bsz, seq, k_bf16, v_blocks_t, lam, bias, gain, out_scale):
    nq = seq // TQ
    assert nq % 2 == 0
    ncol = 2 * N_HEADS * TQ
    return pl.pallas_call(
        functools.partial(_diff_body, out_scale=out_scale),
        grid=(bsz, nq),
        in_specs=[pl.BlockSpec(memory_space=pltpu.SMEM),
                  pl.BlockSpec((TQ, GROUP_W), lambda b, i: (b * nq + i, COL['b_q'])),
                  pl.BlockSpec((None, seq, GROUP_W), lambda b, i: (b, 0, 0)),
                  pl.BlockSpec((None, seq // TK, N_HEADS, 2 * HEAD_DV, TK), lambda b, i: (b, 0, 0, 0, 0)),
                  pl.BlockSpec((4, TQ, ncol), lambda b, i: (0, 0, 0)),
                  pl.BlockSpec((HEAD_DV, TQ), lambda b, i: (0, 0))],
        out_specs=pl.BlockSpec((TQ, GROUP_W), lambda b, i: (b * nq + i, 0)),
        out_shape=jax.ShapeDtypeStruct((bsz * seq, GROUP_W), F32),
        scratch_shapes=[pltpu.VMEM((ncol, GROUP_W), BF16),
                        pltpu.VMEM((1, ncol), F32),
                        pltpu.VMEM((N_HEADS, 2 * HEAD_DV, 2 * TQ), F32)],
        compiler_params=_cparams(("parallel", "parallel")),
        name="diff_prompt",
    )(lam.reshape(1), p2d, k_bf16, v_blocks_t, _bias_tiles_t(bias, 2),
      jnp.broadcast_to(gain[:, None], (HEAD_DV, TQ)))


def _sort_key(score):
    bits = lax.bitcast_convert_type(score, I32)
    return jnp.where(bits < 0, bits ^ 0x7FFFFFFF, bits)


def _kth_largest(count_ge, n_sel, shape):
    t = jnp.full(shape, INT_MIN, I32)
    for bit in range(31, -1, -1):
        step = INT_MIN if bit == 31 else (1 << bit)
        cand = t + jnp.int32(step)
        t = jnp.where(count_ge(cand) >= n_sel, cand, t)
    return t


def _dsa_body(q_ref, qi_ref, wt_ref, k_ref, vt_ref, ki_ref, bias_ref, tri_ref, o_ref,
              qm_ref, qim_ref, key_ref, sel_ref, m_ref, acc_ref, *, n_sel):
    i = pl.program_id(1)
    q = q_ref[...] * (HEAD_DV ** -0.5)
    for h in range(N_HEADS):
        qm_ref[h * TQ:(h + 1) * TQ, :] = _lane_band(q, HEAD_DV * h, HEAD_DV).astype(BF16)
    qi = qi_ref[...]
    for e in range(IDX_HEADS):
        qim_ref[e * TQ:(e + 1) * TQ, :] = _lane_band(qi, IDX_DIM * e, IDX_DIM).astype(BF16)
    w = wt_ref[...] * (IDX_HEADS ** -0.5 * IDX_DIM ** -0.5)
    krow = lax.broadcasted_iota(I32, (TK, TQ), 0)
    qcol = lax.broadcasted_iota(I32, (TK, TQ), 1)
    n_step = (i + 2) // 2

    def score_pair(tt, carry):
        for u in range(2):
            t = jnp.minimum(2 * tt + u, n_step - 1)
            s = jnp.maximum(_dot_nt(ki_ref[pl.ds(pl.multiple_of(t * TK, TK), TK), :], qim_ref[...]), 0.0)
            sc = s[:, 0:TQ] * w[0:1, :]
            for e in range(1, IDX_HEADS):
                sc = sc + s[:, e * TQ:(e + 1) * TQ] * w[e:e + 1, :]
            causal = (krow + t * TK) <= (qcol + i * TQ)
            key_ref[t] = jnp.where(causal, _sort_key(sc), INT_MIN)
        return carry

    lax.fori_loop(0, (n_step + 1) // 2, score_pair, 0)

    def count(pred):
        def body(t, c):
            hit = jnp.where(pred(key_ref[t]), 1.0, 0.0)
            return c + jnp.sum(hit.reshape(TK // COUNT_ROWS, COUNT_ROWS, TQ), axis=0)
        return jnp.sum(lax.fori_loop(0, n_step, body, jnp.zeros((COUNT_ROWS, TQ), F32)), axis=0, keepdims=True)

    thr = _kth_largest(lambda cand: count(lambda k: k >= cand), float(n_sel), (1, TQ))
    no_surplus = jnp.max(count(lambda k: k >= thr)) <= float(n_sel)

    @pl.when(no_surplus)
    def _():
        def select(t, carry):
            sel_ref[t] = jnp.where(key_ref[t] >= thr, 0.0, NEG)
            return carry

        lax.fori_loop(0, n_step, select, 0)

    @pl.when(jnp.logical_not(no_surplus))
    def _():
        room = float(n_sel) - count(lambda k: k > thr)

        def select(t, seen):
            k = key_ref[t]
            eq = jnp.where(k == thr, 1.0, 0.0)
            rank = seen + _dot(tri_ref[...], eq.astype(BF16))
            keep = ((k > thr) | ((k == thr) & (rank <= room))) & (k != INT_MIN)
            sel_ref[t] = jnp.where(keep, 0.0, NEG)
            return seen + jnp.sum(eq, axis=0, keepdims=True)

        lax.fori_loop(0, n_step, select, jnp.zeros((1, TQ), F32))

    m_ref[...] = jnp.full(m_ref.shape, NEG, F32)
    acc_ref[...] = jnp.zeros(acc_ref.shape, F32)

    def attend_pair(tt, carry):
        for u in range(2):
            t = 2 * tt + u
            tc = jnp.minimum(t, n_step - 1)
            sel = sel_ref[tc]
            s = (_dot_nt(k_ref[pl.ds(pl.multiple_of(tc * TK, TK), TK), :], qm_ref[...])
                 + _step_bias(bias_ref, jnp.where(t < n_step, i - 2 * t, -2))
                 + jnp.concatenate([sel] * N_HEADS, axis=1))
            m_old = m_ref[...]
            m_new = jnp.maximum(m_old, jnp.max(s, axis=0, keepdims=True))
            alpha = jnp.exp(m_old - m_new)
            p = jnp.exp(s - m_new).astype(BF16)
            for h in range(N_HEADS):
                cols = slice(h * TQ, (h + 1) * TQ)
                acc_ref[h] = alpha[:, cols] * acc_ref[h] + _dot(vt_ref[tc, h], p[:, cols])
            m_ref[...] = m_new
        return carry

    lax.fori_loop(0, (n_step + 1) // 2, attend_pair, 0)
    out_t = jnp.concatenate([acc_ref[h][:HEAD_DV, :] / acc_ref[h][HEAD_DV:HEAD_DV + 1, :] for h in range(N_HEADS)],
                            axis=0)
    o_ref[...] = jnp.transpose(out_t)


def _dsa_prompt(p2d, bsz, seq, k_bf16, v_blocks_t, ki_rep, w_t, bias):
    nq = seq // TQ
    assert nq % 2 == 0
    n_sel = min(TOPK_MAX, seq // 4)
    tri = jnp.asarray(np.arange(TK)[None, :] <= np.arange(TK)[:, None], BF16)
    qcol = lambda c: pl.BlockSpec((TQ, GROUP_W), lambda b, i, c=c: (b * nq + i, c))
    full = pl.BlockSpec((None, seq, GROUP_W), lambda b, i: (b, 0, 0))
    return pl.pallas_call(
        functools.partial(_dsa_body, n_sel=n_sel),
        grid=(bsz, nq),
        in_specs=[qcol(COL['d_q']), qcol(COL['d_qi']),
                  pl.BlockSpec((IDX_HEADS, TQ), lambda b, i: (0, b * nq + i)),
                  full,
                  pl.BlockSpec((None, seq // TK, N_HEADS, 2 * HEAD_DV, TK), lambda b, i: (b, 0, 0, 0, 0)),
                  full,
                  pl.BlockSpec((4, TQ, N_HEADS * TQ), lambda b, i: (0, 0, 0)),
                  pl.BlockSpec((TK, TK), lambda b, i: (0, 0))],
        out_specs=pl.BlockSpec((TQ, GROUP_W), lambda b, i: (b * nq + i, 0)),
        out_shape=jax.ShapeDtypeStruct((bsz * seq, GROUP_W), F32),
        scratch_shapes=[pltpu.VMEM((N_HEADS * TQ, GROUP_W), BF16),
                        pltpu.VMEM((IDX_HEADS * TQ, GROUP_W), BF16),
                        pltpu.VMEM((seq // TK, TK, TQ), I32),
                        pltpu.VMEM((seq // TK, TK, TQ), F32),
                        pltpu.VMEM((1, N_HEADS * TQ), F32),
                        pltpu.VMEM((N_HEADS, 2 * HEAD_DV, TQ), F32)],
        compiler_params=_cparams(("parallel", "parallel")),
        name="dsa_prompt",
    )(p2d, p2d, w_t, k_bf16, v_blocks_t, ki_rep, _bias_tiles_t(bias, 1), tri)


DEC_Q = 8
PPS_ATTN = 32
PPS_SELECT = 32
PAGE_GROUP = 8


def _paged_update(s, vts, m_ref, l_ref, acc_ref):
    n = len(vts)
    m_old = m_ref[...]
    m_new = jnp.maximum(m_old, jnp.max(s, axis=-1, keepdims=True))
    alpha = jnp.exp(m_old - m_new)
    p = jnp.exp(s - jnp.concatenate([m_new] * n, axis=1))
    l = alpha * l_ref[...]
    acc = jnp.concatenate([alpha, alpha], axis=1) * acc_ref[...]
    for i in range(n):
        pi = p[:, i * PAGE:(i + 1) * PAGE]
        l = l + pi
        acc = acc + _dot_nt(pi.astype(BF16), vts[i])
    l_ref[...] = l
    acc_ref[...] = acc
    m_ref[...] = m_new


def _paged_body(pt_ref, lam_ref, wq_ref, add_ref, knew_ref, vnew_ref, gain_ref, e_ref, *rest,
                n_pages, pps, rows, diff, use_sel, out_scale):
    del pt_ref
    if use_sel:
        sel_ref, selnew_ref = rest[:2]
        rest = rest[2:]
    k_refs, v_refs = rest[:pps], rest[pps:2 * pps]
    o_ref, m_ref, l_ref, acc_ref = rest[2 * pps:]
    g = pl.program_id(1)

    @pl.when(g == 0)
    def _():
        m_ref[...] = jnp.full(m_ref.shape, NEG, F32)
        l_ref[...] = jnp.zeros(l_ref.shape, F32)
        acc_ref[...] = jnp.zeros(acc_ref.shape, F32)

    wq = wq_ref[...]
    last_step = g == n_pages // pps - 1
    far = add_ref[0]
    for lo in range(0, pps, PAGE_GROUP):
        parts = []
        for i in range(lo, min(lo + PAGE_GROUP, pps)):
            si = _dot(wq, k_refs[i][...].astype(BF16))
            si = si + (far if i < pps - 1 else jnp.where(last_step, add_ref[1], far))
            if use_sel:
                si = si + jnp.tile(sel_ref[i], (rows // DEC_Q, 1))
            parts.append(si)
        _paged_update(jnp.concatenate(parts, axis=1),
                      [v_refs[i][...].astype(BF16) for i in range(lo, min(lo + PAGE_GROUP, pps))],
                      m_ref, l_ref, acc_ref)

    @pl.when(last_step)
    def _():
        s = _dot(wq, knew_ref[...]) + add_ref[2]
        if use_sel:
            s = s + jnp.tile(selnew_ref[0], (rows // DEC_Q, 1))
        _paged_update(s, [vnew_ref[...]], m_ref, l_ref, acc_ref)
        o_all = acc_ref[...] / jnp.sum(l_ref[...], axis=-1, keepdims=True)
        out = jnp.zeros((DEC_Q, GROUP_W), F32)
        for h in range(N_HEADS):
            if diff:
                lo = 2 * h * DEC_Q
                o_h = o_all[lo:lo + DEC_Q] - lam_ref[0] * o_all[lo + DEC_Q:lo + 2 * DEC_Q]
            else:
                o_h = o_all[h * DEC_Q:(h + 1) * DEC_Q]
            out = out + _lane_band(o_h, HEAD_DV * h, HEAD_DV)
        if diff:
            sq = out * out
            hi = sq.astype(BF16)
            lo_ = (sq - hi.astype(F32)).astype(BF16)
            msq = (_dot(hi, e_ref[...]) + _dot(lo_, e_ref[...])) * (1.0 / HEAD_DV)
            out = out * lax.rsqrt(msq + EPS) * gain_ref[...] * out_scale
        o_ref[...] = out


def _paged_attn(page_table, lam, wq, add, k_new, v_new, gain256, sel, cache_k, cache_v, l, *, diff, out_scale):
    nreq, n_pages = page_table.shape
    rows = wq.shape[1]
    n_pool = cache_k.shape[1]
    ck = _feature_major_pages(cache_k)
    cv = _feature_major_pages(cache_v)
    use_sel = sel is not None
    pps = min(PPS_ATTN, n_pages)
    page_spec = lambda i: pl.BlockSpec((None, None, GROUP_W, PAGE),
                                       lambda r, g, pt, i=i: (l, pt[r * n_pages + g * pps + i], 0, 0))
    per_req = lambda shape: pl.BlockSpec((None,) + shape, lambda r, g, pt: (r,) + (0,) * len(shape))
    const = lambda shape: pl.BlockSpec(shape, lambda r, g, pt: (0,) * len(shape))
    in_specs = [pl.BlockSpec(memory_space=pltpu.SMEM), per_req((rows, GROUP_W)), const((3, rows, PAGE)),
                per_req((GROUP_W, PAGE)), per_req((GROUP_W, PAGE)), const((1, GROUP_W)), const((GROUP_W, GROUP_W))]
    args = [lam.reshape(1), wq, add, k_new, v_new, gain256.reshape(1, GROUP_W), _hgrn_consts(128)[2]]
    if use_sel:
        in_specs += [pl.BlockSpec((None, pps, DEC_Q, PAGE), lambda r, g, pt: (r, g, 0, 0)),
                     pl.BlockSpec((None, 1, DEC_Q, PAGE), lambda r, g, pt: (r, n_pages, 0, 0))]
        args += [sel, sel]
    in_specs += [page_spec(i) for i in range(pps)] * 2
    args += [ck] * pps + [cv] * pps
    return pl.pallas_call(
        functools.partial(_paged_body, n_pages=n_pages, pps=pps, rows=rows, diff=diff, use_sel=use_sel,
                          out_scale=out_scale),
        grid_spec=pltpu.PrefetchScalarGridSpec(
            num_scalar_prefetch=1,
            grid=(nreq, n_pages // pps),
            in_specs=in_specs,
            out_specs=pl.BlockSpec((DEC_Q, GROUP_W), lambda r, g, pt: (r, 0)),
            scratch_shapes=[pltpu.VMEM((rows, PAGE), F32), pltpu.VMEM((rows, PAGE), F32),
                            pltpu.VMEM((rows, GROUP_W), F32)]),
        out_shape=jax.ShapeDtypeStruct((nreq * DEC_Q, GROUP_W), F32),
        compiler_params=_cparams(("parallel", "arbitrary")),
        name="paged_diff" if diff else "paged_dsa",
    )(page_table.reshape(-1), *args)


def _decode_add_tiles(tab, past, groups_per_head):
    t = np.arange(DEC_Q)[:, None]
    c = np.arange(PAGE)[None, :]
    far = jnp.asarray(np.broadcast_to(past + t - (past - PAGE - 1), (DEC_Q, PAGE)), I32)
    last = jnp.asarray(t + PAGE - c, I32)
    new = jnp.asarray(t - c, I32)
    ok_new = jnp.asarray((c <= t) & (c < DEC_Q))
    tiles = []
    for rel, ok in ((far, None), (last, None), (new, ok_new)):
        b = jnp.transpose(_bucket_lookup(tab, rel), (2, 0, 1))
        if ok is not None:
            b = jnp.where(ok[None], b, NEG)
        tiles.append(jnp.repeat(b, groups_per_head, axis=0).reshape(-1, PAGE))
    return jnp.stack(tiles, axis=0)


def _masked_queries(q, n_groups, width, scale):
    lane = np.arange(GROUP_W)[None, :] // width == np.arange(n_groups)[:, None]
    m = jnp.asarray(lane, F32)[None, :, None, :] * (q * scale)[:, None, :, :]
    return m.reshape(q.shape[0], n_groups * DEC_Q, GROUP_W).astype(BF16)


def _pad_new(a):
    return jnp.pad(jnp.swapaxes(a, 1, 2), ((0, 0), (0, 0), (0, PAGE - DEC_Q))).astype(BF16)


def _feature_major_pages(cache):
    d, n_pool = cache.shape[:2]
    return jnp.swapaxes(cache.reshape(d, n_pool, PAGE, -1), 2, 3)


SEL_CHUNK = 32


def _dsa_sel_body(pt_ref, wqi_ref, wcol_ref, kinew_ref, tri_ref, *rest, n_pages, pps, n_pad, n_sel):
    del pt_ref
    ki_refs = rest[:pps]
    sel_ref, key_ref = rest[pps:]
    g = pl.program_id(1)
    wqi = wqi_ref[...]
    wcol = wcol_ref[...]

    def score(kts):
        kt = kts[0] if len(kts) == 1 else jnp.concatenate(kts, axis=1)
        s = jnp.maximum(_dot(wqi, kt), 0.0) * jnp.concatenate([wcol] * len(kts), axis=1)
        out = s[0:DEC_Q]
        for e in range(1, IDX_HEADS):
            out = out + s[e * DEC_Q:(e + 1) * DEC_Q]
        return out

    keys = _sort_key(score([ki_refs[i][...].astype(BF16) for i in range(pps)]))
    for i in range(pps):
        rows = pl.ds(pl.multiple_of((g * pps + i) * DEC_Q, DEC_Q), DEC_Q)
        key_ref[rows, :] = keys[:, i * PAGE:(i + 1) * PAGE]

    @pl.when(g == n_pages // pps - 1)
    def _():
        t = lax.broadcasted_iota(I32, (DEC_Q, PAGE), 0)
        c = lax.broadcasted_iota(I32, (DEC_Q, PAGE), 1)
        key_ref[n_pages * DEC_Q:(n_pages + 1) * DEC_Q, :] = jnp.where(c <= t, _sort_key(score([kinew_ref[...]])), INT_MIN)
        if n_pad > n_pages + 1:
            key_ref[(n_pages + 1) * DEC_Q:, :] = jnp.full(((n_pad - n_pages - 1) * DEC_Q, PAGE), INT_MIN, I32)

        def count(pred):
            tot = jnp.zeros((COUNT_ROWS, PAGE), F32)
            for ch in range(n_pad // SEL_CHUNK):
                hit = jnp.where(pred(key_ref[ch * SEL_CHUNK * DEC_Q:(ch + 1) * SEL_CHUNK * DEC_Q, :]), 1.0, 0.0)
                tot = tot + jnp.sum(hit.reshape(SEL_CHUNK * DEC_Q // COUNT_ROWS, COUNT_ROWS, PAGE), axis=0)
            tot = jnp.sum(tot.reshape(COUNT_ROWS // DEC_Q, DEC_Q, PAGE), axis=0)
            return jnp.sum(tot, axis=-1, keepdims=True)

        tile_rows = lambda x: jnp.tile(jnp.broadcast_to(x, (DEC_Q, PAGE)), (SEL_CHUNK, 1))
        thr = _kth_largest(lambda cand: count(lambda k, cb=tile_rows(cand): k >= cb), float(n_sel), (DEC_Q, 1))
        n_ge = count(lambda k, tb=tile_rows(thr): k >= tb)
        no_surplus = jnp.max(n_ge) <= float(n_sel)

        @pl.when(no_surplus)
        def _():
            thr_b = jnp.broadcast_to(thr, (DEC_Q, PAGE))
            for j in range(n_pages + 1):
                sel_ref[j] = jnp.where(key_ref[j * DEC_Q:(j + 1) * DEC_Q, :] >= thr_b, 0.0, NEG)

        @pl.when(jnp.logical_not(no_surplus))
        def _():
            room = float(n_sel) - count(lambda k, tb=tile_rows(thr): k > tb)

            def select(j, seen):
                k = key_ref[pl.ds(pl.multiple_of(j * DEC_Q, DEC_Q), DEC_Q), :]
                eq = jnp.where(k == thr, 1.0, 0.0)
                rank = seen + _dot(eq.astype(BF16), tri_ref[...])
                keep = (k > thr) | ((k == thr) & (rank <= room))
                sel_ref[j] = jnp.where(keep, 0.0, NEG)
                return seen + jnp.sum(eq, axis=-1, keepdims=True)

            lax.fori_loop(0, n_pages + 1, select, jnp.zeros((DEC_Q, 1), F32))


def _dsa_decode_select(page_table, wqi, wcol, ki_new, cache_ki, l):
    nreq, n_pages = page_table.shape
    n_sel = min(TOPK_MAX, (n_pages * PAGE + DEC_Q) // 4)
    n_pad = -(-(n_pages + 1) // SEL_CHUNK) * SEL_CHUNK
    tri = jnp.asarray(np.arange(PAGE)[:, None] <= np.arange(PAGE)[None, :], BF16)
    rows = IDX_HEADS * DEC_Q
    pps = min(PPS_SELECT, n_pages)
    page_spec = lambda i: pl.BlockSpec((None, None, IDX_DIM, PAGE),
                                       lambda r, g, pt, i=i: (l, pt[r * n_pages + g * pps + i], 0, 0))
    cache_ki = _feature_major_pages(cache_ki)
    return pl.pallas_call(
        functools.partial(_dsa_sel_body, n_pages=n_pages, pps=pps, n_pad=n_pad, n_sel=n_sel),
        grid_spec=pltpu.PrefetchScalarGridSpec(
            num_scalar_prefetch=1,
            grid=(nreq, n_pages // pps),
            in_specs=[pl.BlockSpec((None, rows, IDX_DIM), lambda r, g, pt: (r, 0, 0)),
                      pl.BlockSpec((None, rows, PAGE), lambda r, g, pt: (r, 0, 0)),
                      pl.BlockSpec((None, IDX_DIM, PAGE), lambda r, g, pt: (r, 0, 0)),
                      pl.BlockSpec((PAGE, PAGE), lambda r, g, pt: (0, 0))]
            + [page_spec(i) for i in range(pps)],
            out_specs=pl.BlockSpec((None, n_pages + 1, DEC_Q, PAGE), lambda r, g, pt: (r, 0, 0, 0)),
            scratch_shapes=[pltpu.VMEM((n_pad * DEC_Q, PAGE), I32)]),
        out_shape=jax.ShapeDtypeStruct((nreq, n_pages + 1, DEC_Q, PAGE), F32),
        compiler_params=_cparams(("parallel", "arbitrary")),
        name="dsa_select",
    )(page_table.reshape(-1), wqi, wcol, ki_new, tri, *([cache_ki] * pps))


def _layer_params(l, prm):
    f32 = F32
    lb_all = jnp.cumsum(jax.nn.softmax(prm['hgrn_lb_logits'].astype(f32), axis=0), axis=0)
    lb = lb_all[l] - lb_all[0]
    lam_init = 0.8 - 0.6 * math.exp(-0.3 * l)
    lam = (jnp.exp(jnp.sum(prm['lq1'][l] * prm['lk1'][l])) - jnp.exp(jnp.sum(prm['lq2'][l] * prm['lk2'][l])) + lam_init)
    w_in = jnp.pad(prm['w_in'][l], ((0, 0), (0, D_IN_PAD - D_IN))).astype(BF16)
    return dict(
        w_in=w_in, log_lb=jnp.maximum(jnp.log(lb), NEG), log1m_lb=jnp.log1p(-lb), lam=lam.astype(f32), lam_init=lam_init,
        bias_b=_bias_tiles(prm['rel_bias'][:, :N_HEADS]), bias_d=_bias_tiles(prm['rel_bias'][:, N_HEADS:]),
        w_out=prm['w_out'][l].astype(BF16), w_gu=prm['w_gate_up'][l].astype(BF16), w_down=prm['w_down'][l].astype(BF16))


def _layer_prompt(x2d, bsz, seq, l, prm, lp, final):
    n = bsz * seq
    p = _proj(x2d, prm['attn_norm'][l], lp['w_in'])
    cols = lambda name: p[:, GROUP_W * COL[name]:GROUP_W * (COL[name] + 1)]
    b_k, b_v, d_k, d_v = cols('b_k'), cols('b_v'), cols('d_k'), cols('d_v')
    d_ki = p[:, 128 * TAIL_COL128:128 * TAIL_COL128 + IDX_DIM]

    st0 = jnp.zeros((bsz, GROUP_W, GROUP_W), F32)
    y_a, st = _hgrn(p, bsz, seq, lp['log_lb'], lp['log1m_lb'], prm['hgrn_norm'][l], st0, seq)

    y_b = _diff_prompt(p, bsz, seq, b_k.astype(BF16).reshape(bsz, seq, GROUP_W), _value_blocks_t(b_v, bsz, seq, TK),
                       lp['lam'], lp['bias_b'], prm['diff_norm'][l], 1.0 - lp['lam_init'])

    mix, bias_tile = _sgu_params(prm['sgu_w'][l], prm['sgu_b'][l], seq)
    y_c, c_v = _sgu(p, prm['sgu_norm'][l], mix, bias_tile)

    ki_rep = jnp.tile(d_ki, (1, IDX_HEADS)).astype(BF16).reshape(bsz, seq, GROUP_W)
    w_t = jnp.transpose(p[:, 128 * TAIL_COL128 + IDX_DIM:128 * TAIL_COL128 + IDX_DIM + IDX_HEADS])
    y_d = _dsa_prompt(p, bsz, seq, d_k.astype(BF16).reshape(bsz, seq, GROUP_W), _value_blocks_t(d_v, bsz, seq, TK),
                      ki_rep, w_t, lp['bias_d'])

    x_new = _post(x2d, (y_a, y_b, y_c, y_d), prm['mix_scale'][l], lp['w_out'], prm['ffn_norm'][l], lp['w_gu'],
                  lp['w_down'], prm['final_norm'], final)
    hd = lambda a: a.reshape(bsz, seq, N_HEADS, HEAD_DV)
    return x_new, (hd(b_k), hd(b_v), hd(d_k), hd(d_v), d_ki.reshape(bsz, seq, IDX_DIM), _blockdiag_to_state(st),
                   c_v.reshape(bsz, seq, GROUP_W))


def _layer_decode(x2d, nreq, l, prm, lp, past, final):
    n = nreq * DEC_Q
    pt = past['page_table']
    past_len = pt.shape[1] * PAGE
    p = _proj(x2d, prm['attn_norm'][l], lp['w_in'])
    cols = lambda name: p[:, GROUP_W * COL[name]:GROUP_W * (COL[name] + 1)]
    per_req = lambda a: a.reshape(nreq, DEC_Q, a.shape[-1])
    b_k, b_v, d_k, d_v = cols('b_k'), cols('b_v'), cols('d_k'), cols('d_v')
    tail = p[:, 128 * TAIL_COL128:128 * (TAIL_COL128 + 1)]
    d_ki = tail[:, :IDX_DIM]

    p16 = jnp.pad(per_req(p), ((0, 0), (0, HG - DEC_Q), (0, 0))).reshape(nreq * HG, D_IN_PAD)
    y_a16, st = _hgrn(p16, nreq, HG, lp['log_lb'], lp['log1m_lb'], prm['hgrn_norm'][l],
                      _state_to_blockdiag(past['hgrn'][l]), DEC_Q)
    y_a = y_a16.reshape(nreq, HG, GROUP_W)[:, :DEC_Q].reshape(n, GROUP_W)

    mix, bias_tile = _sgu_params(prm['sgu_w'][l], prm['sgu_b'][l], DEC_Q)
    y_c, c_v = _sgu(p, prm['sgu_norm'][l], mix, bias_tile)

    y_b = _paged_attn(pt, lp['lam'], _masked_queries(per_req(cols('b_q')), 2 * N_HEADS, DIFF_DQ, DIFF_DQ ** -0.5),
                      _decode_add_tiles(prm['rel_bias'][:, :N_HEADS], past_len, 2), _pad_new(per_req(b_k)),
                      _pad_new(per_req(b_v)), jnp.tile(prm['diff_norm'][l], N_HEADS), None,
                      past['k_diff'], past['v_diff'], l, diff=True, out_scale=1.0 - lp['lam_init'])

    qi = per_req(cols('d_qi')).reshape(nreq, DEC_Q, IDX_HEADS, IDX_DIM)
    wqi = jnp.swapaxes(qi, 1, 2).reshape(nreq, IDX_HEADS * DEC_Q, IDX_DIM).astype(BF16)
    w = per_req(tail[:, IDX_DIM:IDX_DIM + IDX_HEADS] * (IDX_HEADS ** -0.5 * IDX_DIM ** -0.5))
    wcol = jnp.broadcast_to(jnp.swapaxes(w, 1, 2).reshape(nreq, IDX_HEADS * DEC_Q, 1), (nreq, IDX_HEADS * DEC_Q, PAGE))
    sel = _dsa_decode_select(pt, wqi, wcol, _pad_new(per_req(d_ki)), past['k_index'], l)
    y_d = _paged_attn(pt, lp['lam'], _masked_queries(per_req(cols('d_q')), N_HEADS, HEAD_DV, HEAD_DV ** -0.5),
                      _decode_add_tiles(prm['rel_bias'][:, N_HEADS:], past_len, 1), _pad_new(per_req(d_k)),
                      _pad_new(per_req(d_v)), jnp.ones((GROUP_W,), F32), sel,
                      past['k_sparse'], past['v_sparse'], l, diff=False, out_scale=1.0)

    x_new = _post(x2d, (y_a, y_b, y_c, y_d), prm['mix_scale'][l], lp['w_out'], prm['ffn_norm'][l], lp['w_gu'],
                  lp['w_down'], prm['final_norm'], final)
    hd = lambda a: a.reshape(nreq, DEC_Q, N_HEADS, HEAD_DV)
    return x_new, (hd(b_k), hd(b_v), hd(d_k), hd(d_v), d_ki.reshape(nreq, DEC_Q, IDX_DIM), _blockdiag_to_state(st),
                   c_v.reshape(nreq, DEC_Q, GROUP_W))


def kernel(x_prompt, x_sample, cache_k_diff, cache_v_diff, cache_k_sparse, cache_v_sparse, cache_k_index, state_hgrn, page_table, w_in, w_out, attn_norm, ffn_norm, final_norm, w_gate_up, w_down, hgrn_lb_logits, hgrn_norm, diff_lambda_q1, diff_lambda_k1, diff_lambda_q2, diff_lambda_k2, diff_norm, sgu_w, sgu_b, sgu_norm, mix_scale, rel_bias):
    prm = dict(w_in=w_in, w_out=w_out, attn_norm=attn_norm, ffn_norm=ffn_norm, final_norm=final_norm,
               w_gate_up=w_gate_up, w_down=w_down, hgrn_lb_logits=hgrn_lb_logits, hgrn_norm=hgrn_norm,
               lq1=diff_lambda_q1, lk1=diff_lambda_k1, lq2=diff_lambda_q2, lk2=diff_lambda_k2, diff_norm=diff_norm,
               sgu_w=sgu_w, sgu_b=sgu_b, sgu_norm=sgu_norm, mix_scale=mix_scale, rel_bias=rel_bias)
    past = dict(page_table=page_table, k_diff=cache_k_diff, v_diff=cache_v_diff, k_sparse=cache_k_sparse,
                v_sparse=cache_v_sparse, k_index=cache_k_index, hgrn=state_hgrn)
    depth = w_in.shape[0]
    bsz, seq, d = x_prompt.shape
    nreq = x_sample.shape[0]
    xp = x_prompt.reshape(bsz * seq, d)
    xs = x_sample.reshape(nreq * DEC_Q, d)
    st_p, st_s = [], []
    for l in range(depth):
        lp = _layer_params(l, prm)
        final = l == depth - 1
        xp, sp = _layer_prompt(xp, bsz, seq, l, prm, lp, final)
        xs, ss = _layer_decode(xs, nreq, l, prm, lp, past, final)
        st_p.append(sp)
        st_s.append(ss)
    stk = lambda lst, i: jnp.stack([s[i] for s in lst], axis=0)
    return (xp.reshape(bsz, seq, d), xs.reshape(nreq, DEC_Q, d),
            stk(st_p, 0), stk(st_p, 1), stk(st_p, 2), stk(st_p, 3), stk(st_p, 4), stk(st_p, 5),
            stk(st_s, 0), stk(st_s, 1), stk(st_s, 2), stk(st_s, 3), stk(st_s, 4), stk(st_s, 5), stk(st_s, 6))
```
